```python
import jax, jax.numpy as jnp
from jax import lax
import numpy as np

D_MODEL = 1024
BATCH = 4
SEQ = 8192
DEPTH = 2
DEC_BATCH = 128
DEC_SEQ = 1
PAST_LEN = 16384
PAGE_SIZE = 128

A_HEADS = 4
A_DK = 128
A_DV = 128
A_K = A_HEADS * A_DK
A_V = A_HEADS * A_DV
A_CHUNK = 64

POOL_WINDOWS = (2, 4, 8, 16)
POOL_GROUPS = len(POOL_WINDOWS)
POOL_GW = 128
POOL_W = POOL_GROUPS * POOL_GW
POOL_BUF = max(POOL_WINDOWS) - 1

M_HEADS = 4
Q_LORA = 384
KV_LORA = 256
NOPE = 128
ROPE = 64
M_VD = 128
M_V = M_HEADS * M_VD
ROPE_BASE = 10000.0
ATTN_SCALE = (NOPE + ROPE) ** -0.5
Q_BLOCK = 128

N_BRANCH = 3
IN_SIZES = (A_K, A_K, A_V, A_V, POOL_W, Q_LORA, KV_LORA, ROPE, N_BRANCH * D_MODEL)
IN_COLS = sum(IN_SIZES)

D_FF = 2816
N_EXPERTS = 8
TOP_K = 2
N_DENSE = (DEPTH + 1) // 2
N_MOE = DEPTH // 2

ALPHA = (2 * DEPTH) ** 0.25
BETA = (8 * DEPTH) ** -0.25
LN_EPS = 1e-5
RMS_EPS = 1e-6

kernel_name = 'hybrid_hgrn2_pool_mla_decoder_step'


def _layernorm(x, g, b):
    x32 = x.astype(jnp.float32)
    mu = x32.mean(-1, keepdims=True)
    var = jnp.square(x32 - mu).mean(-1, keepdims=True)
    return ((x32 - mu) * lax.rsqrt(var + LN_EPS)).astype(x.dtype) * g + b


def _rmsnorm(x, g):
    x32 = x.astype(jnp.float32)
    return (x32 * lax.rsqrt(jnp.mean(x32 * x32, -1, keepdims=True) + RMS_EPS)).astype(x.dtype) * g


def _rope(x, pos):
    half = x.shape[-1] // 2
    inv = ROPE_BASE ** (-jnp.arange(half, dtype=jnp.float32) / half)
    ang = pos.astype(jnp.float32)[:, None] * inv[None, :]
    ang = ang.reshape((ang.shape[0],) + (1,) * (x.ndim - 3) + (half,))
    c, s = jnp.cos(ang), jnp.sin(ang)
    x32 = x.astype(jnp.float32)
    x1, x2 = x32[..., :half], x32[..., half:]
    return jnp.concatenate([x1 * c - x2 * s, x1 * s + x2 * c], -1).astype(x.dtype)


def _hgrn2_recurrence(q, k, v, logf, s0):
    B, L, H, DK = q.shape
    DV = v.shape[-1]
    c = min(A_CHUNK, L)
    n = -(-L // c)
    pad = n * c - L

    def blocks(t):
        t = jnp.pad(t, ((0, 0), (0, pad), (0, 0), (0, 0)))
        return t.reshape(B, n, c, H, t.shape[-1]).transpose(1, 0, 2, 3, 4)

    causal = jnp.tril(jnp.ones((c, c), bool))[None, :, :, None, None]

    def step(S, inp):
        qc, kc, vc, gc = inp
        b = jnp.cumsum(gc, axis=1)
        o_inter = jnp.einsum('bthk,bhkv->bthv', qc * jnp.exp(b), S)
        decay = jnp.exp(jnp.where(causal, b[:, :, None] - b[:, None, :], -jnp.inf))
        attn = jnp.einsum('bthk,btshk,bshk->bhts', qc, decay, kc)
        o_intra = jnp.einsum('bhts,bshv->bthv', attn, vc)
        b_last = b[:, -1]
        S = jnp.exp(b_last)[..., None] * S + jnp.einsum(
            'bshk,bshv->bhkv', kc * jnp.exp(b_last[:, None] - b), vc)
        return S, o_inter + o_intra

    S, o = lax.scan(step, s0, (blocks(q), blocks(k), blocks(v), blocks(logf)))
    o = o.transpose(1, 0, 2, 3, 4).reshape(B, n * c, H, DV)[:, :L]
    return o, S


def _pool_mixer(p, buf, pos, pool_w, pool_scale):
    B, L, _ = p.shape
    zc = jnp.concatenate([buf.astype(p.dtype), p], axis=1)
    z = zc.astype(jnp.float32)
    cs = jnp.concatenate([jnp.zeros((B, 1, POOL_W), jnp.float32), jnp.cumsum(z, axis=1)], axis=1)
    end = cs[:, POOL_BUF + 1:]
    outs = []
    for gi, w in enumerate(POOL_WINDOWS):
        sl = slice(gi * POOL_GW, (gi + 1) * POOL_GW)
        start = cs[:, POOL_BUF + 1 - w:POOL_BUF + 1 - w + L, sl]
        cnt = jnp.minimum(pos + 1, w).astype(jnp.float32)[None, :, None]
        outs.append((end[..., sl] - start) / cnt - z[:, POOL_BUF:, sl])
    m = jnp.concatenate(outs, axis=-1).reshape(B, L, POOL_GROUPS, POOL_GW)
    m = jnp.einsum('blgc,gcd->blgd', m.astype(p.dtype), pool_w).reshape(B, L, POOL_W) * pool_scale
    return m, zc[:, -POOL_BUF:]


def _mla_prompt(q_lat, q_rope, c_kv, k_rope):
    B, L, H, C = q_lat.shape
    blk = min(Q_BLOCK, L)
    nb = L // blk
    qb_lat = q_lat.reshape(B, nb, blk, H, C).transpose(1, 0, 2, 3, 4)
    qb_rope = q_rope.reshape(B, nb, blk, H, ROPE).transpose(1, 0, 2, 3, 4)
    kpos = jnp.arange(L)

    def one(args):
        ql, qr, bi = args
        s = (jnp.einsum('bqhc,bkc->bhqk', ql, c_kv)
             + jnp.einsum('bqhr,bkr->bhqk', qr, k_rope)).astype(jnp.float32) * ATTN_SCALE
        qpos = bi * blk + jnp.arange(blk)
        s = jnp.where(kpos[None, :] <= qpos[:, None], s, -jnp.inf)
        pr = jax.nn.softmax(s, axis=-1).astype(c_kv.dtype)
        return jnp.einsum('bhqk,bkc->bqhc', pr, c_kv)

    o = lax.map(one, (qb_lat, qb_rope, jnp.arange(nb)))
    return o.transpose(1, 0, 2, 3, 4).reshape(B, L, H, C)


def _mla_sample(q_lat, q_rope, c_kv, k_rope, past_ckv, past_kr):
    L = q_lat.shape[1]
    P = past_ckv.shape[1]
    s_past = (jnp.einsum('bqhc,bkc->bhqk', q_lat, past_ckv)
              + jnp.einsum('bqhr,bkr->bhqk', q_rope, past_kr)).astype(jnp.float32)
    s_new = (jnp.einsum('bqhc,bkc->bhqk', q_lat, c_kv)
             + jnp.einsum('bqhr,bkr->bhqk', q_rope, k_rope)).astype(jnp.float32)
    s_new = jnp.where(jnp.tril(jnp.ones((L, L), bool)), s_new, -jnp.inf)
    pr = jax.nn.softmax(jnp.concatenate([s_past, s_new], -1) * ATTN_SCALE, axis=-1).astype(c_kv.dtype)
    return (jnp.einsum('bhqk,bkc->bqhc', pr[..., :P], past_ckv)
            + jnp.einsum('bhqk,bkc->bqhc', pr[..., P:], c_kv))


def _token_mixers(u, pos, hgrn_s0, pool_buf, kv_past, lw):
    B, L, _ = u.shape
    h = u @ lw['w_in']
    q, f, i, g, p, cq, ckv, kr, gate = jnp.split(h, np.cumsum(IN_SIZES)[:-1].tolist(), axis=-1)

    lb = lw['lb']
    f32 = f.astype(jnp.float32)
    logf = jnp.logaddexp(jnp.log(lb), jnp.log1p(-lb) + jax.nn.log_sigmoid(f32))
    kk = (1.0 - lb) * jax.nn.sigmoid(-f32)
    qh = jax.nn.silu(q.astype(jnp.float32))
    o, s_new = _hgrn2_recurrence(
        qh.reshape(B, L, A_HEADS, A_DK), kk.reshape(B, L, A_HEADS, A_DK),
        i.astype(jnp.float32).reshape(B, L, A_HEADS, A_DV), logf.reshape(B, L, A_HEADS, A_DK),
        hgrn_s0.astype(jnp.float32))
    o = _rmsnorm(o, lw['a_norm']) * jax.nn.sigmoid(g.astype(jnp.float32)).reshape(B, L, A_HEADS, A_DV)
    hgrn_branch = o.astype(u.dtype).reshape(B, L, A_V) @ lw['w_a_out']

    pm, pool_new = _pool_mixer(p, pool_buf, pos, lw['pool_w'], lw['pool_scale'])
    pool_branch = pm @ lw['w_p_out']

    cq = _rmsnorm(cq, lw['q_norm'])
    qf = (cq @ lw['w_uq']).reshape(B, L, M_HEADS, NOPE + ROPE)
    q_nope = qf[..., :NOPE]
    q_rope = _rope(qf[..., NOPE:], pos)
    c_kv = _rmsnorm(ckv, lw['kv_norm'])
    k_rope = _rope(kr, pos)
    q_lat = jnp.einsum('blhn,chn->blhc', q_nope, lw['w_uk'])
    if kv_past is None:
        o_lat = _mla_prompt(q_lat, q_rope, c_kv, k_rope)
    else:
        o_lat = _mla_sample(q_lat, q_rope, c_kv, k_rope, kv_past[0], kv_past[1])
    mla_branch = jnp.einsum('blhc,chv->blhv', o_lat, lw['w_uv']).reshape(B, L, M_V) @ lw['w_m_out']

    gt = jax.nn.sigmoid(gate.reshape(B, L, N_BRANCH, D_MODEL) + lw['b_gate'])
    merged = gt[:, :, 0] * hgrn_branch + gt[:, :, 1] * pool_branch + gt[:, :, 2] * mla_branch
    return merged @ lw['w_o'], c_kv, k_rope, s_new.astype(u.dtype), pool_new


def _swiglu(x, w1, w3, w2):
    return (jax.nn.silu(x @ w1) * (x @ w3)) @ w2


def _moe(x, router, w1, w3, w2):
    logits = (x @ router).astype(jnp.float32)
    top_v, top_i = lax.top_k(logits, TOP_K)
    top_w = jax.nn.softmax(top_v, axis=-1)
    gates = jnp.einsum('blk,blke->ble', top_w,
                       jax.nn.one_hot(top_i, N_EXPERTS, dtype=jnp.float32)).astype(x.dtype)
    out = jnp.zeros_like(x)
    for e in range(N_EXPERTS):
        out = out + gates[..., e:e + 1] * _swiglu(x, w1[e], w3[e], w2[e])
    return out


def setup_inputs(seed: int = 0) -> dict:
    key = jax.random.key(seed)
    ks = iter(jax.random.split(key, 40))

    def nrm(shape, scale=1.0):
        return jax.random.normal(next(ks), shape, jnp.float32) * scale

    n_pages = PAST_LEN // PAGE_SIZE
    n_used = DEC_BATCH * n_pages
    n_pool = n_used + (n_used + 3) // 4
    x_prompt = nrm((BATCH, SEQ, D_MODEL))
    x_sample = nrm((DEC_BATCH, DEC_SEQ, D_MODEL))
    cache_ckv = nrm((DEPTH, n_pool, PAGE_SIZE, KV_LORA))
    cache_krope = nrm((DEPTH, n_pool, PAGE_SIZE, ROPE))
    state_hgrn = nrm((DEPTH, DEC_BATCH, A_HEADS, A_DK, A_DV))
    state_pool = nrm((DEPTH, DEC_BATCH, POOL_BUF, POOL_W))
    page_table = jax.random.permutation(next(ks), n_pool)[:n_used].reshape(DEC_BATCH, n_pages).astype(jnp.int32)
    return {
        'x_prompt': x_prompt,
        'x_sample': x_sample,
        'cache_ckv': cache_ckv,
        'cache_krope': cache_krope,
        'state_hgrn': state_hgrn,
        'state_pool': state_pool,
        'page_table': page_table,
        'w_in': nrm((DEPTH, D_MODEL, IN_COLS), D_MODEL ** -0.5),
        'b_gate': nrm((DEPTH, N_BRANCH, D_MODEL), 0.1),
        'hgrn_lb_logits': nrm((DEPTH, A_K)),
        'hgrn_norm': 1.0 + nrm((DEPTH, A_DV), 0.02),
        'w_hgrn_out': nrm((DEPTH, A_V, D_MODEL), A_V ** -0.5),
        'pool_w': nrm((DEPTH, POOL_GROUPS, POOL_GW, POOL_GW), POOL_GW ** -0.5),
        'pool_scale': 1.0 + nrm((DEPTH, POOL_W), 0.1),
        'w_pool_out': nrm((DEPTH, POOL_W, D_MODEL), POOL_W ** -0.5),
        'mla_q_norm': 1.0 + nrm((DEPTH, Q_LORA), 0.02),
        'w_uq': nrm((DEPTH, Q_LORA, M_HEADS * (NOPE + ROPE)), Q_LORA ** -0.5),
        'mla_kv_norm': 1.0 + nrm((DEPTH, KV_LORA), 0.02),
        'w_uk': nrm((DEPTH, KV_LORA, M_HEADS, NOPE), KV_LORA ** -0.5),
        'w_uv': nrm((DEPTH, KV_LORA, M_HEADS, M_VD), KV_LORA ** -0.5),
        'w_mla_out': nrm((DEPTH, M_V, D_MODEL), M_V ** -0.5),
        'w_o': nrm((DEPTH, D_MODEL, D_MODEL), D_MODEL ** -0.5 * BETA),
        'ln1_g': 1.0 + nrm((DEPTH, D_MODEL), 0.02),
        'ln1_b': nrm((DEPTH, D_MODEL), 0.02),
        'ln2_g': 1.0 + nrm((DEPTH, D_MODEL), 0.02),
        'ln2_b': nrm((DEPTH, D_MODEL), 0.02),
        'ffn_w1': nrm((N_DENSE, D_MODEL, D_FF), D_MODEL ** -0.5),
        'ffn_w3': nrm((N_DENSE, D_MODEL, D_FF), D_MODEL ** -0.5),
        'ffn_w2': nrm((N_DENSE, D_FF, D_MODEL), D_FF ** -0.5 * BETA),
        'moe_router': nrm((N_MOE, D_MODEL, N_EXPERTS), D_MODEL ** -0.5),
        'moe_w1': nrm((N_MOE, N_EXPERTS, D_MODEL, D_FF), D_MODEL ** -0.5),
        'moe_w3': nrm((N_MOE, N_EXPERTS, D_MODEL, D_FF), D_MODEL ** -0.5),
        'moe_w2': nrm((N_MOE, N_EXPERTS, D_FF, D_MODEL), D_FF ** -0.5 * BETA),
    }


def reference(x_prompt, x_sample, cache_ckv, cache_krope, state_hgrn, state_pool, page_table,
              w_in, b_gate, hgrn_lb_logits, hgrn_norm, w_hgrn_out, pool_w, pool_scale, w_pool_out,
              mla_q_norm, w_uq, mla_kv_norm, w_uk, w_uv, w_mla_out, w_o,
              ln1_g, ln1_b, ln2_g, ln2_b, ffn_w1, ffn_w3, ffn_w2,
              moe_router, moe_w1, moe_w3, moe_w2):
    Bp, Lp, _ = x_prompt.shape
    Bs, Ls, _ = x_sample.shape
    pos_p = jnp.arange(Lp, dtype=jnp.int32)
    pos_s = PAST_LEN + jnp.arange(Ls, dtype=jnp.int32)
    lb_cum = jnp.cumsum(jax.nn.softmax(hgrn_lb_logits.astype(jnp.float32), axis=0), axis=0)

    hp, hs = x_prompt, x_sample
    ckv_p, kr_p, ckv_s, kr_s = [], [], [], []
    sh_p, sh_s, pb_p, pb_s = [], [], [], []
    for l in range(DEPTH):
        lw = {
            'w_in': w_in[l], 'b_gate': b_gate[l], 'lb': lb_cum[l] - lb_cum[0],
            'a_norm': hgrn_norm[l], 'w_a_out': w_hgrn_out[l],
            'pool_w': pool_w[l], 'pool_scale': pool_scale[l], 'w_p_out': w_pool_out[l],
            'q_norm': mla_q_norm[l], 'w_uq': w_uq[l], 'kv_norm': mla_kv_norm[l],
            'w_uk': w_uk[l], 'w_uv': w_uv[l], 'w_m_out': w_mla_out[l], 'w_o': w_o[l],
        }
        mp, c1, k1, s1, b1 = _token_mixers(
            hp, pos_p, jnp.zeros((Bp, A_HEADS, A_DK, A_DV), jnp.float32),
            jnp.zeros((Bp, POOL_BUF, POOL_W), hp.dtype), None, lw)
        past = (cache_ckv[l, page_table].reshape(Bs, -1, KV_LORA),
                cache_krope[l, page_table].reshape(Bs, -1, ROPE))
        ms, c2, k2, s2, b2 = _token_mixers(hs, pos_s, state_hgrn[l], state_pool[l], past, lw)
        ckv_p.append(c1); kr_p.append(k1); sh_p.append(s1); pb_p.append(b1)
        ckv_s.append(c2); kr_s.append(k2); sh_s.append(s2); pb_s.append(b2)

        hp = _layernorm(ALPHA * hp + mp, ln1_g[l], ln1_b[l])
        hs = _layernorm(ALPHA * hs + ms, ln1_g[l], ln1_b[l])
        j = l // 2
        if l % 2 == 0:
            fp = _swiglu(hp, ffn_w1[j], ffn_w3[j], ffn_w2[j])
            fs = _swiglu(hs, ffn_w1[j], ffn_w3[j], ffn_w2[j])
        else:
            fp = _moe(hp, moe_router[j], moe_w1[j], moe_w3[j], moe_w2[j])
            fs = _moe(hs, moe_router[j], moe_w1[j], moe_w3[j], moe_w2[j])
        hp = _layernorm(ALPHA * hp + fp, ln2_g[l], ln2_b[l])
        hs = _layernorm(ALPHA * hs + fs, ln2_g[l], ln2_b[l])

    return (hp, hs, jnp.stack(ckv_p), jnp.stack(kr_p), jnp.stack(ckv_s), jnp.stack(kr_s),
            jnp.stack(sh_p), jnp.stack(sh_s), jnp.stack(pb_p), jnp.stack(pb_s))
```

```python
import functools
import math

import numpy as np
import jax
import jax.numpy as jnp
from jax import lax
from jax.experimental import pallas as pl
from jax.experimental.pallas import tpu as pltpu

D_MODEL = 1024
A_HEADS = 4
A_DK = 128
A_DV = 128
A_K = A_HEADS * A_DK
A_V = A_HEADS * A_DV
POOL_WINDOWS = (2, 4, 8, 16)
POOL_GW = 128
POOL_W = len(POOL_WINDOWS) * POOL_GW
POOL_BUF = max(POOL_WINDOWS) - 1
M_HEADS = 4
Q_LORA = 384
KV_LORA = 256
NOPE = 128
ROPE = 64
M_VD = 128
M_V = M_HEADS * M_VD
QK_CAT = KV_LORA + ROPE
ROPE_BASE = 10000.0
ATTN_SCALE = (NOPE + ROPE) ** -0.5
N_BRANCH = 3
D_FF = 2816
N_EXPERTS = 8
LN_EPS = 1e-5
RMS_EPS = 1e-6
LOG2E = 1.4426950408889634

BF16 = jnp.bfloat16
F32 = jnp.float32
VMEM_LIMIT = 56 * 2 ** 20

_NT = (((1,), (1,)), ((), ()))
_TN = (((0,), (0,)), ((), ()))


def _params(*sem):
    return pltpu.CompilerParams(dimension_semantics=sem, vmem_limit_bytes=VMEM_LIMIT)


def _tile(n, pref):
    return pref if n % pref == 0 else n


def _dot(a, b):
    return jnp.dot(a, b, preferred_element_type=F32)


def _sigmoid(x):
    return 1.0 / (1.0 + jnp.exp(-x))


def _layernorm(y, g, b):
    mu = jnp.mean(y, axis=-1, keepdims=True)
    d = y - mu
    var = jnp.mean(d * d, axis=-1, keepdims=True)
    return d * lax.rsqrt(var + LN_EPS) * g + b


def _rms(x):
    return x * lax.rsqrt(jnp.mean(x * x, axis=-1, keepdims=True) + RMS_EPS)


def _mm_kernel(x_ref, w_ref, o_ref):
    o_ref[...] = _dot(x_ref[...], w_ref[...]).astype(o_ref.dtype)


def _mm_gate_kernel(x_ref, w_ref, b_ref, o_ref):
    o_ref[...] = _sigmoid(_dot(x_ref[...], w_ref[...]) + b_ref[...]).astype(o_ref.dtype)


def _matmul(x, w, out_dtype, tn, bias=None):
    n, k = x.shape
    m = w.shape[1]
    tm = _tile(n, 1024)
    in_specs = [pl.BlockSpec((tm, k), lambda i, j: (i, 0)), pl.BlockSpec((k, tn), lambda i, j: (0, j))]
    args = [x, w]
    body = _mm_kernel
    if bias is not None:
        in_specs.append(pl.BlockSpec((1, tn), lambda i, j: (0, j)))
        args.append(bias)
        body = _mm_gate_kernel
    return pl.pallas_call(
        body, grid=(n // tm, m // tn), in_specs=in_specs,
        out_specs=pl.BlockSpec((tm, tn), lambda i, j: (i, j)),
        out_shape=jax.ShapeDtypeStruct((n, m), out_dtype),
        compiler_params=_params("parallel", "arbitrary"))(*args)


HGRN_CHUNK = 128


def _hgrn_tables(c):
    nlev = int(math.log2(c))
    t = np.arange(c)
    u = t[None, :]
    mats = np.zeros((nlev + 2, c, c), np.float32)
    lvl = -np.ones((c, c), np.int32)
    for l in range(nlev):
        blk = c >> l
        s0 = (t // blk) * blk
        mid = s0 + blk // 2
        qrole = t >= mid
        row_q = (u > mid[:, None]) & (u <= t[:, None])
        row_k = (u > t[:, None]) & (u <= mid[:, None])
        mats[l] = np.where(qrole[:, None], row_q, row_k)
        pair = (s0[:, None] == s0[None, :]) & qrole[:, None] & (~qrole)[None, :]
        lvl[pair] = l
    lvl[t, t] = nlev
    mats[nlev] = u <= t[:, None]
    mats[nlev + 1] = u > t[:, None]
    return mats.reshape((nlev + 2) * c, c), lvl, nlev


def _hgrn_gates(f, lb):
    log_sig = jnp.minimum(f, 0.0) - jnp.log1p(jnp.exp(-jnp.abs(f)))
    b = jnp.log1p(-lb) + log_sig
    a = jnp.log(jnp.maximum(lb, 1e-37))
    both = jnp.maximum(a, b) + jnp.log1p(jnp.exp(-jnp.abs(a - b)))
    logf = jnp.where(lb > 0.0, both, b)
    kk = (1.0 - lb) * _sigmoid(-f)
    return logf, kk


def _hgrn_kernel(q_ref, f_ref, i_ref, g_ref, lb_ref, an_ref, mall_ref, lvl_ref, og_ref, s_ref, st_scr,
                 *, c, nlev):
    ci = pl.program_id(1)

    @pl.when(ci == 0)
    def _():
        st_scr[...] = jnp.zeros_like(st_scr)

    lvl = lvl_ref[...]
    mall = mall_ref[...]
    an = an_ref[...]
    for h in range(A_HEADS):
        sl = slice(h * A_DK, (h + 1) * A_DK)
        q = q_ref[:, sl]
        qh = q * _sigmoid(q)
        logf, kk = _hgrn_gates(f_ref[:, sl], lb_ref[:, sl])
        v = i_ref[:, sl].astype(BF16)
        hi = logf.astype(BF16)
        mid = (logf - hi.astype(F32)).astype(BF16)
        e = jnp.exp(_dot(mall, hi) + _dot(mall, mid))
        a = jnp.zeros((c, c), F32)
        for l in range(nlev):
            el = e[l * c:(l + 1) * c]
            p = lax.dot_general((qh * el).astype(BF16), (kk * el).astype(BF16), _NT,
                                preferred_element_type=F32)
            a = jnp.where(lvl == l, p, a)
        p = lax.dot_general(qh.astype(BF16), kk.astype(BF16), _NT, preferred_element_type=F32)
        a = jnp.where(lvl == nlev, p, a)
        eb = e[nlev * c:(nlev + 1) * c]
        ee = e[(nlev + 1) * c:]
        st = st_scr[h]
        o = _dot(a.astype(BF16), v) + lax.dot_general((qh * eb).astype(BF16), st.astype(BF16), _NT,
                                                      preferred_element_type=F32)
        st_new = st * eb[c - 1:c, :] + lax.dot_general(v, (kk * ee).astype(BF16), _TN,
                                                       preferred_element_type=F32)
        st_scr[h] = st_new
        og_ref[:, sl] = (_rms(o) * an * _sigmoid(g_ref[:, sl])).astype(og_ref.dtype)

        @pl.when(ci == pl.num_programs(1) - 1)
        def _():
            s_ref[0, h] = st_new.T


def _hgrn_prompt(hh, lb, a_norm, batch, seq):
    c = _tile(seq, HGRN_CHUNK)
    mall, lvl, nlev = _hgrn_tables(c)
    nc = seq // c
    col = lambda k: pl.BlockSpec((c, A_K), lambda b, i, k=k: (b * nc + i, k))
    const = lambda shape: pl.BlockSpec(shape, lambda b, i: (0,) * len(shape))
    return pl.pallas_call(
        functools.partial(_hgrn_kernel, c=c, nlev=nlev),
        grid=(batch, nc),
        in_specs=[col(0), col(1), col(2), col(3), const((1, A_K)), const((1, A_DV)),
                  const(mall.shape), const(lvl.shape)],
        out_specs=[pl.BlockSpec((c, A_V), lambda b, i: (b * nc + i, 0)),
                   pl.BlockSpec((1, A_HEADS, A_DK, A_DV), lambda b, i: (b, 0, 0, 0))],
        out_shape=[jax.ShapeDtypeStruct((batch * seq, A_V), BF16),
                   jax.ShapeDtypeStruct((batch, A_HEADS, A_DK, A_DV), F32)],
        scratch_shapes=[pltpu.VMEM((A_HEADS, A_DV, A_DK), F32)],
        compiler_params=_params("parallel", "arbitrary"),
    )(hh, hh, hh, hh, lb, a_norm, jnp.asarray(mall, BF16), jnp.asarray(lvl))


HGRN_STEP_ROWS = 8


def _hgrn_step_kernel(h_ref, s0_ref, lb_ref, an_ref, og_ref, s_ref):
    rows = HGRN_STEP_ROWS
    rid = lax.broadcasted_iota(jnp.int32, (rows, A_DK), 0)
    ones = jnp.ones((rows, A_DV), F32)
    an = an_ref[...]
    hp = lax.Precision.HIGHEST
    for h in range(A_HEADS):
        sl = lambda k: slice(k * A_K + h * A_DK, k * A_K + (h + 1) * A_DK)
        q = h_ref[:, sl(0)]
        qh = q * _sigmoid(q)
        f = h_ref[:, sl(1)]
        lb = lb_ref[:, h * A_DK:(h + 1) * A_DK]
        fg = lb + (1.0 - lb) * _sigmoid(f)
        kk = (1.0 - lb) * _sigmoid(-f)
        v = h_ref[:, sl(2)]
        o = jnp.zeros((rows, A_DV), F32)
        for r in range(rows):
            pick = rid == r
            fcol = lax.dot_general(jnp.where(pick, fg, 0.0), ones, _TN, precision=hp, preferred_element_type=F32)
            upd = lax.dot_general(jnp.where(pick, kk, 0.0), v, _TN, precision=hp, preferred_element_type=F32)
            s_new = fcol * s0_ref[0, r, h] + upd
            s_ref[r, h] = s_new
            o = o + jnp.dot(jnp.where(pick, qh, 0.0), s_new, precision=hp, preferred_element_type=F32)
        og_ref[:, h * A_DV:(h + 1) * A_DV] = (_rms(o) * an * _sigmoid(h_ref[:, sl(3)])).astype(og_ref.dtype)


def _hgrn_step(hh, state, layer, lb, a_norm):
    bs = hh.shape[0]
    rows = HGRN_STEP_ROWS
    return pl.pallas_call(
        _hgrn_step_kernel, grid=(bs // rows,),
        in_specs=[pl.BlockSpec((rows, 4 * A_K), lambda i: (i, 0)),
                  pl.BlockSpec((1, rows, A_HEADS, A_DK, A_DV), lambda i: (layer, i, 0, 0, 0)),
                  pl.BlockSpec((1, A_K), lambda i: (0, 0)), pl.BlockSpec((1, A_DV), lambda i: (0, 0))],
        out_specs=[pl.BlockSpec((rows, A_V), lambda i: (i, 0)),
                   pl.BlockSpec((rows, A_HEADS, A_DK, A_DV), lambda i: (i, 0, 0, 0))],
        out_shape=[jax.ShapeDtypeStruct((bs, A_V), BF16),
                   jax.ShapeDtypeStruct((bs, A_HEADS, A_DK, A_DV), F32)],
        compiler_params=_params("parallel"),
    )(hh, state, lb, a_norm)


POOL_HALO = 16


def _pool_project(m_groups, pw_ref, ps_ref, o_ref):
    for gi, m in enumerate(m_groups):
        sl = slice(gi * POOL_GW, (gi + 1) * POOL_GW)
        o_ref[:, sl] = (_dot(m.astype(BF16), pw_ref[gi]) * ps_ref[:, sl]).astype(o_ref.dtype)


def _pool_kernel(p_ref, pw_ref, ps_ref, o_ref, prev_scr, *, t):
    ti = pl.program_id(1)

    @pl.when(ti == 0)
    def _():
        prev_scr[...] = jnp.zeros_like(prev_scr)

    z = p_ref[...]
    zext = jnp.concatenate([prev_scr[...], z], axis=0)
    prev_scr[...] = z[t - POOL_HALO:, :]
    pos1 = (ti * t + 1 + lax.broadcasted_iota(jnp.int32, (t, 1), 0)).astype(F32)
    s = zext
    groups = []
    for gi, w in enumerate(POOL_WINDOWS):
        s = s[:, (POOL_GW if gi else 0):]
        s = s + pltpu.roll(s, w // 2, 0)
        sl = slice(gi * POOL_GW, (gi + 1) * POOL_GW)
        groups.append(s[POOL_HALO:, :POOL_GW] / jnp.minimum(pos1, float(w)) - z[:, sl])
    _pool_project(groups, pw_ref, ps_ref, o_ref)


def _pool_prompt(p, pool_w, pool_scale, batch, seq):
    t = _tile(seq, 512)
    nt = seq // t
    return pl.pallas_call(
        functools.partial(_pool_kernel, t=t), grid=(batch, nt),
        in_specs=[pl.BlockSpec((t, POOL_W), lambda b, i: (b * nt + i, 0)),
                  pl.BlockSpec(pool_w.shape, lambda b, i: (0, 0, 0)),
                  pl.BlockSpec((1, POOL_W), lambda b, i: (0, 0))],
        out_specs=pl.BlockSpec((t, POOL_W), lambda b, i: (b * nt + i, 0)),
        out_shape=jax.ShapeDtypeStruct((batch * seq, POOL_W), BF16),
        scratch_shapes=[pltpu.VMEM((POOL_HALO, POOL_W), F32)],
        compiler_params=_params("parallel", "arbitrary"),
    )(p, pool_w, pool_scale)


def _pool_step_kernel(p_ref, buf_ref, pw_ref, ps_ref, o_ref, nb_ref, *, past_len):
    z = p_ref[...]
    groups = []
    for gi, w in enumerate(POOL_WINDOWS):
        s = z[:, gi * POOL_GW:(gi + 1) * POOL_GW]
        for r in range(POOL_BUF - (w - 1), POOL_BUF):
            s = s + buf_ref[0, :, r * POOL_W + gi * POOL_GW:r * POOL_W + (gi + 1) * POOL_GW]
        groups.append(s / float(min(past_len + 1, w)) - z[:, gi * POOL_GW:(gi + 1) * POOL_GW])
    _pool_project(groups, pw_ref, ps_ref, o_ref)
    nb_ref[:, :(POOL_BUF - 1) * POOL_W] = buf_ref[0, :, POOL_W:]
    nb_ref[:, (POOL_BUF - 1) * POOL_W:] = z


def _pool_step(p, state_pool, layer, pool_w, pool_scale, past_len):
    bs = p.shape[0]
    flat = POOL_BUF * POOL_W
    pm, nb = pl.pallas_call(
        functools.partial(_pool_step_kernel, past_len=past_len), grid=(1,),
        in_specs=[pl.BlockSpec((bs, POOL_W), lambda i: (0, 0)),
                  pl.BlockSpec((1, bs, flat), lambda i: (layer, 0, 0)),
                  pl.BlockSpec(pool_w.shape, lambda i: (0, 0, 0)),
                  pl.BlockSpec((1, POOL_W), lambda i: (0, 0))],
        out_specs=[pl.BlockSpec((bs, POOL_W), lambda i: (0, 0)),
                   pl.BlockSpec((bs, flat), lambda i: (0, 0))],
        out_shape=[jax.ShapeDtypeStruct((bs, POOL_W), BF16),
                   jax.ShapeDtypeStruct((bs, flat), F32)],
        compiler_params=_params("arbitrary"),
    )(p, state_pool.reshape(state_pool.shape[0], bs, flat), pool_w, pool_scale)
    return pm, nb.reshape(bs, POOL_BUF, POOL_W)


MLA_IN = Q_LORA + KV_LORA + 2 * ROPE
Q_HEAD_COLS = NOPE + 2 * ROPE


def _rope(pair, cs):
    prod = pair * cs
    return prod[:, :ROPE] + prod[:, ROPE:]


def _mla_prep_kernel(h_ref, cs_ref, qn_ref, kn_ref, wuq_ref, wuk_ref, ckv_ref, kr_ref, kcat_ref, qcat_ref):
    cs = cs_ref[...]
    c_kv = _rms(h_ref[:, Q_LORA:Q_LORA + KV_LORA]) * kn_ref[...]
    k_rope = _rope(h_ref[:, Q_LORA + KV_LORA:], cs)
    ckv_ref[...] = c_kv
    kr_ref[...] = k_rope
    kcat_ref[:, :KV_LORA] = c_kv.astype(BF16)
    kcat_ref[:, KV_LORA:] = k_rope.astype(BF16)
    cq = (_rms(h_ref[:, :Q_LORA]) * qn_ref[...]).astype(BF16)
    qf = _dot(cq, wuq_ref[...])
    qscale = ATTN_SCALE * LOG2E
    for h in range(M_HEADS):
        base = h * Q_HEAD_COLS
        q_lat = _dot(qf[:, base:base + NOPE].astype(BF16), wuk_ref[h])
        q_rope = _rope(qf[:, base + NOPE:base + Q_HEAD_COLS], cs)
        qcat_ref[0, h, :, :KV_LORA] = (q_lat * qscale).astype(BF16)
        qcat_ref[0, h, :, KV_LORA:] = (q_rope * qscale).astype(BF16)


def _mla_prep(hm, cs, q_norm, kv_norm, wuq, wuk_t, batch, seq):
    t = _tile(seq, 512)
    nt = seq // t
    n = batch * seq
    row = lambda w: pl.BlockSpec((t, w), lambda b, i: (b * nt + i, 0))
    const = lambda shape: pl.BlockSpec(shape, lambda b, i: (0,) * len(shape))
    return pl.pallas_call(
        _mla_prep_kernel, grid=(batch, nt),
        in_specs=[row(MLA_IN), pl.BlockSpec((t, 2 * ROPE), lambda b, i: (i, 0)),
                  const((1, Q_LORA)), const((1, KV_LORA)), const(wuq.shape), const(wuk_t.shape)],
        out_specs=[row(KV_LORA), row(ROPE), row(QK_CAT),
                   pl.BlockSpec((1, M_HEADS, t, QK_CAT), lambda b, i: (b, 0, i, 0))],
        out_shape=[jax.ShapeDtypeStruct((n, KV_LORA), F32), jax.ShapeDtypeStruct((n, ROPE), F32),
                   jax.ShapeDtypeStruct((n, QK_CAT), BF16),
                   jax.ShapeDtypeStruct((batch, M_HEADS, seq, QK_CAT), BF16)],
        compiler_params=_params("parallel", "parallel"),
    )(hm, cs, q_norm, kv_norm, wuq, wuk_t)


def _flash_kernel(q_ref, k_ref, wuv_ref, o_ref, m_scr, l_scr, acc_scr, *, bq):
    qi = pl.program_id(1)
    rows = M_HEADS * bq
    q = q_ref[0].reshape(rows, QK_CAT)
    m_scr[...] = jnp.full_like(m_scr, -jnp.inf)
    l_scr[...] = jnp.zeros_like(l_scr)
    acc_scr[...] = jnp.zeros_like(acc_scr)

    def step(j, masked):
        kblk = k_ref[0, pl.ds(pl.multiple_of(j * bq, bq), bq), :]
        s = lax.dot_general(q, kblk, _NT, preferred_element_type=F32)
        if masked:
            tq = lax.broadcasted_iota(jnp.int32, (M_HEADS, bq, bq), 1).reshape(rows, bq)
            tk = lax.broadcasted_iota(jnp.int32, (rows, bq), 1)
            s = jnp.where(tk <= tq, s, -jnp.inf)
        m_prev = m_scr[...]
        m_new = jnp.maximum(m_prev, jnp.max(s, axis=1, keepdims=True))
        alpha = jnp.exp2(m_prev - m_new)
        p = jnp.exp2(s - m_new)
        l_scr[...] = alpha * l_scr[...] + jnp.sum(p, axis=1, keepdims=True)
        acc_scr[...] = alpha * acc_scr[...] + _dot(p.astype(BF16), kblk[:, :KV_LORA])
        m_scr[...] = m_new

    def body(j, carry):
        step(j, False)
        return carry

    lax.fori_loop(0, qi, body, 0)
    step(qi, True)
    inv_l = 1.0 / l_scr[...]
    for h in range(M_HEADS):
        o_lat = (acc_scr[h * bq:(h + 1) * bq, :] * inv_l[h * bq:(h + 1) * bq]).astype(BF16)
        o_ref[:, h * M_VD:(h + 1) * M_VD] = _dot(o_lat, wuv_ref[h]).astype(o_ref.dtype)


def _flash_prompt(qcat, kcat, wuv, batch, seq):
    bq = _tile(seq, 512)
    nq = seq // bq
    rows = M_HEADS * bq
    return pl.pallas_call(
        functools.partial(_flash_kernel, bq=bq), grid=(batch, nq),
        in_specs=[pl.BlockSpec((1, M_HEADS, bq, QK_CAT), lambda b, i: (b, 0, i, 0)),
                  pl.BlockSpec((1, seq, QK_CAT), lambda b, i: (b, 0, 0)),
                  pl.BlockSpec(wuv.shape, lambda b, i: (0, 0, 0))],
        out_specs=pl.BlockSpec((bq, M_V), lambda b, i: (b * nq + i, 0)),
        out_shape=jax.ShapeDtypeStruct((batch * seq, M_V), BF16),
        scratch_shapes=[pltpu.VMEM((rows, 1), F32), pltpu.VMEM((rows, 1), F32),
                        pltpu.VMEM((rows, KV_LORA), F32)],
        compiler_params=_params("parallel", "arbitrary"),
    )(qcat, kcat.reshape(batch, seq, QK_CAT), wuv)


Q_ROWS = 8
PAGES_PER_STEP = 32


def _decode_kernel(pt_ref, q_ref, cnew_ref, rnew_ref, wuv_ref, ckv_hbm, kr_hbm, o_ref,
                   kbuf, rbuf, sem, m_scr, l_scr, acc_scr, *, layer, ppc, page):
    b = pl.program_id(0)
    c = pl.program_id(1)
    nb = pl.num_programs(0)
    nch = pl.num_programs(1)
    step = b * nch + c
    slot = step % 2

    def copies(bb, cc, sl, j):
        pg = pt_ref[bb, cc * ppc + j]
        rows = pl.ds(j * page, page)
        return (pltpu.make_async_copy(ckv_hbm.at[layer, pg], kbuf.at[sl, rows], sem.at[0, sl]),
                pltpu.make_async_copy(kr_hbm.at[layer, pg], rbuf.at[sl, rows], sem.at[1, sl]))

    def issue(bb, cc, sl):
        for j in range(ppc):
            for cp in copies(bb, cc, sl, j):
                cp.start()

    @pl.when(step == 0)
    def _():
        issue(b, c, slot)

    @pl.when(step + 1 < nb * nch)
    def _():
        nxt = step + 1
        issue(nxt // nch, nxt % nch, 1 - slot)

    q = q_ref[0]
    q_lat = q[:, :KV_LORA]
    q_rope = q[:, KV_LORA:]

    @pl.when(c == 0)
    def _():
        c_new = cnew_ref[0]
        m_scr[...] = (jnp.sum(q_lat.astype(F32) * c_new, axis=1, keepdims=True)
                      + jnp.sum(q_rope.astype(F32) * rnew_ref[0], axis=1, keepdims=True))
        l_scr[...] = jnp.ones_like(l_scr)
        acc_scr[...] = jnp.broadcast_to(c_new, acc_scr.shape)

    for j in range(ppc):
        for cp in copies(b, c, slot, j):
            cp.wait()

    kc = kbuf[slot].astype(BF16)
    kr = rbuf[slot].astype(BF16)
    s = (lax.dot_general(q_lat, kc, _NT, preferred_element_type=F32)
         + lax.dot_general(q_rope, kr, _NT, preferred_element_type=F32))
    m_prev = m_scr[...]
    m_new = jnp.maximum(m_prev, jnp.max(s, axis=1, keepdims=True))
    alpha = jnp.exp2(m_prev - m_new)
    p = jnp.exp2(s - m_new)
    l_scr[...] = alpha * l_scr[...] + jnp.sum(p, axis=1, keepdims=True)
    acc_scr[...] = alpha * acc_scr[...] + _dot(p.astype(BF16), kc)
    m_scr[...] = m_new

    @pl.when(c == nch - 1)
    def _():
        o_lat = (acc_scr[...] / l_scr[...]).astype(BF16)
        for h in range(M_HEADS):
            o_ref[0, :, h * M_VD:(h + 1) * M_VD] = _dot(o_lat, wuv_ref[h])[h:h + 1].astype(o_ref.dtype)


def _mla_decode(page_table, q8, c_new, r_new, wuv, cache_ckv, cache_krope, layer):
    bs, n_pages = page_table.shape
    page = cache_ckv.shape[2]
    ppc = PAGES_PER_STEP if n_pages % PAGES_PER_STEP == 0 else n_pages
    nch = n_pages // ppc
    grid_spec = pltpu.PrefetchScalarGridSpec(
        num_scalar_prefetch=1, grid=(bs, nch),
        in_specs=[pl.BlockSpec((1, Q_ROWS, QK_CAT), lambda b, c, pt: (b, 0, 0)),
                  pl.BlockSpec((1, 1, KV_LORA), lambda b, c, pt: (b, 0, 0)),
                  pl.BlockSpec((1, 1, ROPE), lambda b, c, pt: (b, 0, 0)),
                  pl.BlockSpec(wuv.shape, lambda b, c, pt: (0, 0, 0)),
                  pl.BlockSpec(memory_space=pl.ANY), pl.BlockSpec(memory_space=pl.ANY)],
        out_specs=pl.BlockSpec((1, 1, M_V), lambda b, c, pt: (b, 0, 0)),
        scratch_shapes=[pltpu.VMEM((2, ppc * page, KV_LORA), F32), pltpu.VMEM((2, ppc * page, ROPE), F32),
                        pltpu.SemaphoreType.DMA((2, 2)),
                        pltpu.VMEM((Q_ROWS, 1), F32), pltpu.VMEM((Q_ROWS, 1), F32),
                        pltpu.VMEM((Q_ROWS, KV_LORA), F32)])
    return pl.pallas_call(
        functools.partial(_decode_kernel, layer=layer, ppc=ppc, page=page),
        grid_spec=grid_spec, out_shape=jax.ShapeDtypeStruct((bs, 1, M_V), BF16),
        compiler_params=_params("arbitrary", "arbitrary"),
    )(page_table, q8, c_new, r_new, wuv, cache_ckv, cache_krope)


def _merge_kernel(og_ref, pm_ref, ov_ref, gt_ref, x_ref, wa_ref, wp_ref, wm_ref, wo_ref, g_ref, b_ref,
                  o_ref, ob_ref, *, alpha):
    d = D_MODEL
    merged = (gt_ref[:, :d].astype(F32) * _dot(og_ref[...], wa_ref[...])
              + gt_ref[:, d:2 * d].astype(F32) * _dot(pm_ref[...], wp_ref[...])
              + gt_ref[:, 2 * d:].astype(F32) * _dot(ov_ref[...], wm_ref[...]))
    y = alpha * x_ref[...] + _dot(merged.astype(BF16), wo_ref[...])
    out = _layernorm(y, g_ref[...], b_ref[...])
    o_ref[...] = out
    ob_ref[...] = out.astype(BF16)


def _merge(og, pm, ov, gates, x, wa, wp, wm, wo, g, b, alpha):
    n = x.shape[0]
    t = _tile(n, 512)
    row = lambda w: pl.BlockSpec((t, w), lambda i: (i, 0))
    const = lambda a: pl.BlockSpec(a.shape, lambda i: (0,) * a.ndim)
    return pl.pallas_call(
        functools.partial(_merge_kernel, alpha=alpha), grid=(n // t,),
        in_specs=[row(A_V), row(POOL_W), row(M_V), row(N_BRANCH * D_MODEL), row(D_MODEL),
                  const(wa), const(wp), const(wm), const(wo), const(g), const(b)],
        out_specs=[row(D_MODEL), row(D_MODEL)],
        out_shape=[jax.ShapeDtypeStruct((n, D_MODEL), F32), jax.ShapeDtypeStruct((n, D_MODEL), BF16)],
        compiler_params=_params("parallel"),
    )(og, pm, ov, gates, x, wa, wp, wm, wo, g, b)


FF_BLOCK = D_FF // 2


def _swiglu_block(xb, w1, w3, w2):
    a = _dot(xb, w1)
    return _dot((a * _sigmoid(a) * _dot(xb, w3)).astype(BF16), w2)


def _ffn_kernel(xb_ref, x_ref, w1_ref, w3_ref, w2_ref, g_ref, b_ref, o_ref, ob_ref, acc_scr, *, alpha):
    j = pl.program_id(1)

    @pl.when(j == 0)
    def _():
        acc_scr[...] = jnp.zeros_like(acc_scr)

    acc_scr[...] += _swiglu_block(xb_ref[...], w1_ref[...], w3_ref[...], w2_ref[...])

    @pl.when(j == pl.num_programs(1) - 1)
    def _():
        out = _layernorm(alpha * x_ref[...] + acc_scr[...], g_ref[...], b_ref[...])
        o_ref[...] = out
        ob_ref[...] = out.astype(BF16)


def _ffn(xb, x, w1, w3, w2, g, b, alpha):
    n = x.shape[0]
    t = _tile(n, 512)
    nf = D_FF // FF_BLOCK
    row = lambda: pl.BlockSpec((t, D_MODEL), lambda i, j: (i, 0))
    vec = lambda: pl.BlockSpec((1, D_MODEL), lambda i, j: (0, 0))
    return pl.pallas_call(
        functools.partial(_ffn_kernel, alpha=alpha), grid=(n // t, nf),
        in_specs=[row(), row(), pl.BlockSpec((D_MODEL, FF_BLOCK), lambda i, j: (0, j)),
                  pl.BlockSpec((D_MODEL, FF_BLOCK), lambda i, j: (0, j)),
                  pl.BlockSpec((FF_BLOCK, D_MODEL), lambda i, j: (j, 0)), vec(), vec()],
        out_specs=[row(), row()],
        out_shape=[jax.ShapeDtypeStruct((n, D_MODEL), F32), jax.ShapeDtypeStruct((n, D_MODEL), BF16)],
        scratch_shapes=[pltpu.VMEM((t, D_MODEL), F32)],
        compiler_params=_params("parallel", "arbitrary"),
    )(xb, x, w1, w3, w2, g, b)


def _router_kernel(x_ref, r_ref, o_ref):
    logits = jnp.dot(x_ref[...], r_ref[...], precision=lax.Precision.HIGHEST, preferred_element_type=F32)
    lane = lax.broadcasted_iota(jnp.int32, logits.shape, 1)
    m1 = jnp.max(logits, axis=1, keepdims=True)
    i1 = jnp.min(jnp.where(logits == m1, lane, N_EXPERTS), axis=1, keepdims=True)
    rest = jnp.where(lane == i1, -jnp.inf, logits)
    m2 = jnp.max(rest, axis=1, keepdims=True)
    i2 = jnp.min(jnp.where(rest == m2, lane, N_EXPERTS), axis=1, keepdims=True)
    e2 = jnp.exp(m2 - m1)
    o_ref[...] = jnp.where(lane == i1, 1.0 / (1.0 + e2), 0.0) + jnp.where(lane == i2, e2 / (1.0 + e2), 0.0)


def _router(x, router):
    n = x.shape[0]
    t = _tile(n, 1024)
    return pl.pallas_call(
        _router_kernel, grid=(n // t,),
        in_specs=[pl.BlockSpec((t, D_MODEL), lambda i: (i, 0)),
                  pl.BlockSpec((D_MODEL, N_EXPERTS), lambda i: (0, 0))],
        out_specs=pl.BlockSpec((t, N_EXPERTS), lambda i: (i, 0)),
        out_shape=jax.ShapeDtypeStruct((n, N_EXPERTS), F32),
        compiler_params=_params("parallel"),
    )(x, router)


def _moe_kernel(xb_ref, x_ref, gt_ref, w1_ref, w3_ref, w2_ref, g_ref, b_ref, o_ref, ob_ref, acc_scr, *, alpha):
    e = pl.program_id(1)
    j = pl.program_id(2)

    @pl.when((e == 0) & (j == 0))
    def _():
        acc_scr[...] = jnp.zeros_like(acc_scr)

    gates = gt_ref[...]
    lane = lax.broadcasted_iota(jnp.int32, gates.shape, 1)
    gate = jnp.sum(jnp.where(lane == e, gates, 0.0), axis=1, keepdims=True)
    acc_scr[...] += gate * _swiglu_block(xb_ref[...], w1_ref[0], w3_ref[0], w2_ref[0])

    @pl.when((e == pl.num_programs(1) - 1) & (j == pl.num_programs(2) - 1))
    def _():
        out = _layernorm(alpha * x_ref[...] + acc_scr[...], g_ref[...], b_ref[...])
        o_ref[...] = out
        ob_ref[...] = out.astype(BF16)


def _moe(xb, x, gates, w1, w3, w2, g, b, alpha):
    n = x.shape[0]
    t = _tile(n, 512)
    nf = D_FF // FF_BLOCK
    row = lambda w: pl.BlockSpec((t, w), lambda i, e, j: (i, 0))
    vec = lambda: pl.BlockSpec((1, D_MODEL), lambda i, e, j: (0, 0))
    return pl.pallas_call(
        functools.partial(_moe_kernel, alpha=alpha), grid=(n // t, N_EXPERTS, nf),
        in_specs=[row(D_MODEL), row(D_MODEL), row(N_EXPERTS),
                  pl.BlockSpec((1, D_MODEL, FF_BLOCK), lambda i, e, j: (e, 0, j)),
                  pl.BlockSpec((1, D_MODEL, FF_BLOCK), lambda i, e, j: (e, 0, j)),
                  pl.BlockSpec((1, FF_BLOCK, D_MODEL), lambda i, e, j: (e, j, 0)), vec(), vec()],
        out_specs=[row(D_MODEL), row(D_MODEL)],
        out_shape=[jax.ShapeDtypeStruct((n, D_MODEL), F32), jax.ShapeDtypeStruct((n, D_MODEL), BF16)],
        scratch_shapes=[pltpu.VMEM((t, D_MODEL), F32)],
        compiler_params=_params("parallel", "arbitrary", "arbitrary"),
    )(xb, x, gates, w1, w3, w2, g, b)


def _swap_halves_cols(w):
    half = w.shape[-1] // 2
    return jnp.concatenate([w[..., half:], w[..., :half]], axis=-1)


def _rope_table(pos):
    half = ROPE // 2
    inv = ROPE_BASE ** (-jnp.arange(half, dtype=F32) / half)
    ang = pos.astype(F32)[:, None] * inv[None, :]
    c, s = jnp.cos(ang), jnp.sin(ang)
    return jnp.concatenate([c, c, -s, s], axis=-1)


def _layer_weights(l, w_in, b_gate, lb, hgrn_norm, w_hgrn_out, pool_w, pool_scale, w_pool_out,
                   mla_q_norm, w_uq, mla_kv_norm, w_uk, w_uv, w_mla_out, w_o):
    offs = np.cumsum((0, A_K, A_K, A_V, A_V, POOL_W, Q_LORA, KV_LORA, ROPE))
    wi = w_in[l]
    w_kr = wi[:, offs[7]:offs[8]]
    uq = w_uq[l].reshape(Q_LORA, M_HEADS, NOPE + ROPE)
    uq = jnp.concatenate([uq, _swap_halves_cols(uq[..., NOPE:])], axis=-1)
    return dict(
        w_hgrn=wi[:, :offs[4]].astype(BF16),
        w_pool=wi[:, offs[4]:offs[5]].astype(BF16),
        w_mla=jnp.concatenate([wi[:, offs[5]:offs[8]], _swap_halves_cols(w_kr)], axis=1).astype(BF16),
        w_gate=wi[:, offs[8]:].astype(BF16),
        b_gate=b_gate[l].reshape(1, N_BRANCH * D_MODEL),
        lb=lb[l].reshape(1, A_K),
        a_norm=hgrn_norm[l].reshape(1, A_DV),
        w_a_out=w_hgrn_out[l].astype(BF16),
        pool_w=pool_w[l].astype(BF16),
        pool_scale=pool_scale[l].reshape(1, POOL_W),
        w_p_out=w_pool_out[l].astype(BF16),
        q_norm=mla_q_norm[l].reshape(1, Q_LORA),
        kv_norm=mla_kv_norm[l].reshape(1, KV_LORA),
        w_uq=uq.reshape(Q_LORA, M_HEADS * Q_HEAD_COLS).astype(BF16),
        w_uk_t=jnp.transpose(w_uk[l], (1, 2, 0)).astype(BF16),
        w_uv=jnp.transpose(w_uv[l], (1, 0, 2)).astype(BF16),
        w_m_out=w_mla_out[l].astype(BF16),
        w_o=w_o[l].astype(BF16),
    )


def _in_projections(xb, lw):
    hh = _matmul(xb, lw['w_hgrn'], F32, 1024)
    hp = _matmul(xb, lw['w_pool'], F32, POOL_W)
    hm = _matmul(xb, lw['w_mla'], F32, MLA_IN)
    gates = _matmul(xb, lw['w_gate'], BF16, 1024, bias=lw['b_gate'])
    return hh, hp, hm, gates


def kernel(x_prompt, x_sample, cache_ckv, cache_krope, state_hgrn, state_pool, page_table, w_in, b_gate, hgrn_lb_logits, hgrn_norm, w_hgrn_out, pool_w, pool_scale, w_pool_out, mla_q_norm, w_uq, mla_kv_norm, w_uk, w_uv, w_mla_out, w_o, ln1_g, ln1_b, ln2_g, ln2_b, ffn_w1, ffn_w3, ffn_w2, moe_router, moe_w1, moe_w3, moe_w2):
    bp, lp, _ = x_prompt.shape
    bs, ls, _ = x_sample.shape
    assert ls == 1, "the decode path handles one new token per sequence"
    depth = w_in.shape[0]
    past_len = page_table.shape[1] * cache_ckv.shape[2]
    alpha = (2 * depth) ** 0.25
    npr = bp * lp

    lb_cum = jnp.cumsum(jax.nn.softmax(hgrn_lb_logits.astype(F32), axis=0), axis=0)
    lb = lb_cum - lb_cum[0:1]
    cs_p = _rope_table(jnp.arange(lp, dtype=jnp.int32))
    cs_s = _rope_table(jnp.full((bs,), past_len, jnp.int32))
    vec = lambda a: a.reshape(1, D_MODEL)

    xp = x_prompt.reshape(npr, D_MODEL)
    xs = x_sample.reshape(bs, D_MODEL)
    xpb, xsb = xp.astype(BF16), xs.astype(BF16)
    outs = {k: [] for k in ('ckv_p', 'kr_p', 'ckv_s', 'kr_s', 'sh_p', 'sh_s', 'pb_p', 'pb_s')}
    for l in range(depth):
        lw = _layer_weights(l, w_in, b_gate, lb, hgrn_norm, w_hgrn_out, pool_w, pool_scale, w_pool_out,
                            mla_q_norm, w_uq, mla_kv_norm, w_uk, w_uv, w_mla_out, w_o)
        hh, hp, hm, gates = _in_projections(xpb, lw)
        og, s_p = _hgrn_prompt(hh, lw['lb'], lw['a_norm'], bp, lp)
        pm = _pool_prompt(hp, lw['pool_w'], lw['pool_scale'], bp, lp)
        ckv, kr, kcat, qcat = _mla_prep(hm, cs_p, lw['q_norm'], lw['kv_norm'], lw['w_uq'], lw['w_uk_t'], bp, lp)
        ov = _flash_prompt(qcat, kcat, lw['w_uv'], bp, lp)
        xp1, xp1b = _merge(og, pm, ov, gates, xp, lw['w_a_out'], lw['w_p_out'], lw['w_m_out'], lw['w_o'],
                           vec(ln1_g[l]), vec(ln1_b[l]), alpha)
        outs['ckv_p'].append(ckv.reshape(bp, lp, KV_LORA))
        outs['kr_p'].append(kr.reshape(bp, lp, ROPE))
        outs['sh_p'].append(s_p)
        outs['pb_p'].append(hp.reshape(bp, lp, POOL_W)[:, lp - POOL_BUF:])
        hh, hp, hm, gates = _in_projections(xsb, lw)
        og, s_s = _hgrn_step(hh, state_hgrn, l, lw['lb'], lw['a_norm'])
        pm, pb_s = _pool_step(hp, state_pool, l, lw['pool_w'], lw['pool_scale'], past_len)
        ckv, kr, _, qcat = _mla_prep(hm, cs_s, lw['q_norm'], lw['kv_norm'], lw['w_uq'], lw['w_uk_t'], 1, bs)
        q8 = jnp.pad(jnp.transpose(qcat[0], (1, 0, 2)), ((0, 0), (0, Q_ROWS - M_HEADS), (0, 0)))
        ckv = ckv.reshape(bs, 1, KV_LORA)
        kr = kr.reshape(bs, 1, ROPE)
        ov = _mla_decode(page_table, q8, ckv, kr, lw['w_uv'], cache_ckv, cache_krope, l).reshape(bs, M_V)
        xs1, xs1b = _merge(og, pm, ov, gates, xs, lw['w_a_out'], lw['w_p_out'], lw['w_m_out'], lw['w_o'],
                           vec(ln1_g[l]), vec(ln1_b[l]), alpha)
        outs['ckv_s'].append(ckv)
        outs['kr_s'].append(kr)
        outs['sh_s'].append(s_s)
        outs['pb_s'].append(pb_s)
        j = l // 2
        if l % 2 == 0:
            w1, w3, w2 = ffn_w1[j].astype(BF16), ffn_w3[j].astype(BF16), ffn_w2[j].astype(BF16)
            xp, xpb = _ffn(xp1b, xp1, w1, w3, w2, vec(ln2_g[l]), vec(ln2_b[l]), alpha)
            xs, xsb = _ffn(xs1b, xs1, w1, w3, w2, vec(ln2_g[l]), vec(ln2_b[l]), alpha)
        else:
            w1, w3, w2 = moe_w1[j].astype(BF16), moe_w3[j].astype(BF16), moe_w2[j].astype(BF16)
            xp, xpb = _moe(xp1b, xp1, _router(xp1, moe_router[j]), w1, w3, w2, vec(ln2_g[l]), vec(ln2_b[l]), alpha)
            xs, xsb = _moe(xs1b, xs1, _router(xs1, moe_router[j]), w1, w3, w2, vec(ln2_g[l]), vec(ln2_b[l]), alpha)

    st = lambda k: jnp.stack(outs[k])
    return (xp.reshape(bp, lp, D_MODEL), xs.reshape(bs, ls, D_MODEL), st('ckv_p'), st('kr_p'), st('ckv_s'),
            st('kr_s'), st('sh_p'), st('sh_s'), st('pb_p'), st('pb_s'))
```

```python
import functools
import math

import numpy as np
import jax
import jax.numpy as jnp
from jax import lax
from jax.experimental import pallas as pl
from jax.experimental.pallas import tpu as pltpu

D_MODEL = 1024
A_HEADS = 4
A_DK = 128
A_DV = 128
A_K = A_HEADS * A_DK
A_V = A_HEADS * A_DV
POOL_WINDOWS = (2, 4, 8, 16)
POOL_GW = 128
POOL_W = len(POOL_WINDOWS) * POOL_GW
POOL_BUF = max(POOL_WINDOWS) - 1
M_HEADS = 4
Q_LORA = 384
KV_LORA = 256
NOPE = 128
ROPE = 64
M_VD = 128
M_V = M_HEADS * M_VD
QK_CAT = KV_LORA + ROPE
ROPE_BASE = 10000.0
ATTN_SCALE = (NOPE + ROPE) ** -0.5
N_BRANCH = 3
D_FF = 2816
N_EXPERTS = 8
LN_EPS = 1e-5
RMS_EPS = 1e-6
LOG2E = 1.4426950408889634

BF16 = jnp.bfloat16
F32 = jnp.float32
VMEM_LIMIT = 56 * 2 ** 20

_NT = (((1,), (1,)), ((), ()))
_TN = (((0,), (0,)), ((), ()))


def _params(*sem):
    return pltpu.CompilerParams(dimension_semantics=sem, vmem_limit_bytes=VMEM_LIMIT)


def _tile(n, pref):
    return pref if n % pref == 0 else n


def _dot(a, b):
    return jnp.dot(a, b, preferred_element_type=F32)


def _sigmoid(x):
    return 1.0 / (1.0 + jnp.exp(-x))


def _layernorm(y, g, b):
    mu = jnp.mean(y, axis=-1, keepdims=True)
    d = y - mu
    var = jnp.mean(d * d, axis=-1, keepdims=True)
    return d * lax.rsqrt(var + LN_EPS) * g + b


def _rms(x):
    return x * lax.rsqrt(jnp.mean(x * x, axis=-1, keepdims=True) + RMS_EPS)


def _mm_kernel(x_ref, w_ref, o_ref):
    o_ref[...] = _dot(x_ref[...], w_ref[...]).astype(o_ref.dtype)


def _mm_gate_kernel(x_ref, w_ref, b_ref, o_ref):
    o_ref[...] = _sigmoid(_dot(x_ref[...], w_ref[...]) + b_ref[...]).astype(o_ref.dtype)


def _matmul(x, w, out_dtype, tn, bias=None):
    n, k = x.shape
    m = w.shape[1]
    tm = _tile(n, 1024)
    in_specs = [pl.BlockSpec((tm, k), lambda i, j: (i, 0)), pl.BlockSpec((k, tn), lambda i, j: (0, j))]
    args = [x, w]
    body = _mm_kernel
    if bias is not None:
        in_specs.append(pl.BlockSpec((1, tn), lambda i, j: (0, j)))
        args.append(bias)
        body = _mm_gate_kernel
    return pl.pallas_call(
        body, grid=(n // tm, m // tn), in_specs=in_specs,
        out_specs=pl.BlockSpec((tm, tn), lambda i, j: (i, j)),
        out_shape=jax.ShapeDtypeStruct((n, m), out_dtype),
        compiler_params=_params("parallel", "arbitrary"))(*args)


HGRN_CHUNK = 128


def _hgrn_tables(c):
    nlev = int(math.log2(c))
    t = np.arange(c)
    u = t[None, :]
    mats = np.zeros((nlev + 2, c, c), np.float32)
    lvl = -np.ones((c, c), np.int32)
    for l in range(nlev):
        blk = c >> l
        s0 = (t // blk) * blk
        mid = s0 + blk // 2
        qrole = t >= mid
        row_q = (u > mid[:, None]) & (u <= t[:, None])
        row_k = (u > t[:, None]) & (u <= mid[:, None])
        mats[l] = np.where(qrole[:, None], row_q, row_k)
        pair = (s0[:, None] == s0[None, :]) & qrole[:, None] & (~qrole)[None, :]
        lvl[pair] = l
    lvl[t, t] = nlev
    mats[nlev] = u <= t[:, None]
    mats[nlev + 1] = u > t[:, None]
    return mats.reshape((nlev + 2) * c, c), lvl, nlev


def _hgrn_gates(f, lb):
    log_sig = jnp.minimum(f, 0.0) - jnp.log1p(jnp.exp(-jnp.abs(f)))
    b = jnp.log1p(-lb) + log_sig
    a = jnp.log(jnp.maximum(lb, 1e-37))
    both = jnp.maximum(a, b) + jnp.log1p(jnp.exp(-jnp.abs(a - b)))
    logf = jnp.where(lb > 0.0, both, b)
    kk = (1.0 - lb) * _sigmoid(-f)
    return logf, kk


def _hgrn_kernel(q_ref, f_ref, i_ref, g_ref, lb_ref, an_ref, mall_ref, lvl_ref, og_ref, s_ref, st_scr,
                 *, c, nlev):
    ci = pl.program_id(1)

    @pl.when(ci == 0)
    def _():
        st_scr[...] = jnp.zeros_like(st_scr)

    lvl = lvl_ref[...]
    an = an_ref[...]
    heads = range(A_HEADS)
    hsl = [slice(h * A_DK, (h + 1) * A_DK) for h in heads]
    q = q_ref[...]
    qh = q * _sigmoid(q)
    logf, kk = _hgrn_gates(f_ref[...], lb_ref[...])
    hi = logf.astype(BF16)
    mid = (logf - hi.astype(F32)).astype(BF16)
    d2 = _dot(mall_ref[...], jnp.concatenate([hi, mid], axis=1))
    e = jnp.exp(d2[:, :A_K] + d2[:, A_K:])
    a = [jnp.zeros((c, c), F32) for _ in heads]
    for l in range(nlev + 1):
        for h in heads:
            if l < nlev:
                el = e[l * c:(l + 1) * c, hsl[h]]
                x, y = qh[:, hsl[h]] * el, kk[:, hsl[h]] * el
            else:
                x, y = qh[:, hsl[h]], kk[:, hsl[h]]
            p = lax.dot_general(x.astype(BF16), y.astype(BF16), _NT, preferred_element_type=F32)
            a[h] = jnp.where(lvl == l, p, a[h])
    eb = e[nlev * c:(nlev + 1) * c]
    ke = (kk * e[(nlev + 1) * c:]).astype(BF16)
    qe = (qh * eb).astype(BF16)
    v = i_ref[...].astype(BF16)
    gate = _sigmoid(g_ref[...])
    for h in heads:
        st = st_scr[h]
        o = _dot(a[h].astype(BF16), v[:, hsl[h]]) + lax.dot_general(qe[:, hsl[h]], st.astype(BF16), _NT,
                                                                    preferred_element_type=F32)
        st_new = st * eb[c - 1:c, hsl[h]] + lax.dot_general(v[:, hsl[h]], ke[:, hsl[h]], _TN,
                                                             preferred_element_type=F32)
        st_scr[h] = st_new
        og_ref[:, hsl[h]] = (_rms(o) * an * gate[:, hsl[h]]).astype(og_ref.dtype)

        @pl.when(ci == pl.num_programs(1) - 1)
        def _():
            s_ref[0, h] = st_new.T


def _hgrn_prompt(hh, lb, a_norm, batch, seq):
    c = _tile(seq, HGRN_CHUNK)
    mall, lvl, nlev = _hgrn_tables(c)
    nc = seq // c
    col = lambda k: pl.BlockSpec((c, A_K), lambda b, i, k=k: (b * nc + i, k))
    const = lambda shape: pl.BlockSpec(shape, lambda b, i: (0,) * len(shape))
    return pl.pallas_call(
        functools.partial(_hgrn_kernel, c=c, nlev=nlev),
        grid=(batch, nc),
        in_specs=[col(0), col(1), col(2), col(3), const((1, A_K)), const((1, A_DV)),
                  const(mall.shape), const(lvl.shape)],
        out_specs=[pl.BlockSpec((c, A_V), lambda b, i: (b * nc + i, 0)),
                   pl.BlockSpec((1, A_HEADS, A_DK, A_DV), lambda b, i: (b, 0, 0, 0))],
        out_shape=[jax.ShapeDtypeStruct((batch * seq, A_V), BF16),
                   jax.ShapeDtypeStruct((batch, A_HEADS, A_DK, A_DV), F32)],
        scratch_shapes=[pltpu.VMEM((A_HEADS, A_DV, A_DK), F32)],
        compiler_params=_params("parallel", "arbitrary"),
    )(hh, hh, hh, hh, lb, a_norm, jnp.asarray(mall, BF16), jnp.asarray(lvl))


HGRN_STEP_ROWS = 8


def _hgrn_step_kernel(h_ref, s0_ref, lb_ref, an_ref, og_ref, s_ref):
    rows = HGRN_STEP_ROWS
    rid = lax.broadcasted_iota(jnp.int32, (rows, A_DK), 0)
    ones = jnp.ones((rows, A_DV), F32)
    an = an_ref[...]
    hp = lax.Precision.HIGHEST
    for h in range(A_HEADS):
        sl = lambda k: slice(k * A_K + h * A_DK, k * A_K + (h + 1) * A_DK)
        q = h_ref[:, sl(0)]
        qh = q * _sigmoid(q)
        f = h_ref[:, sl(1)]
        lb = lb_ref[:, h * A_DK:(h + 1) * A_DK]
        fg = lb + (1.0 - lb) * _sigmoid(f)
        kk = (1.0 - lb) * _sigmoid(-f)
        v = h_ref[:, sl(2)]
        o = jnp.zeros((rows, A_DV), F32)
        for r in range(rows):
            pick = rid == r
            fcol = lax.dot_general(jnp.where(pick, fg, 0.0), ones, _TN, precision=hp, preferred_element_type=F32)
            upd = lax.dot_general(jnp.where(pick, kk, 0.0), v, _TN, precision=hp, preferred_element_type=F32)
            s_new = fcol * s0_ref[0, r, h] + upd
            s_ref[r, h] = s_new
            o = o + jnp.dot(jnp.where(pick, qh, 0.0), s_new, precision=hp, preferred_element_type=F32)
        og_ref[:, h * A_DV:(h + 1) * A_DV] = (_rms(o) * an * _sigmoid(h_ref[:, sl(3)])).astype(og_ref.dtype)


def _hgrn_step(hh, state, layer, lb, a_norm):
    bs = hh.shape[0]
    rows = HGRN_STEP_ROWS
    return pl.pallas_call(
        _hgrn_step_kernel, grid=(bs // rows,),
        in_specs=[pl.BlockSpec((rows, 4 * A_K), lambda i: (i, 0)),
                  pl.BlockSpec((1, rows, A_HEADS, A_DK, A_DV), lambda i: (layer, i, 0, 0, 0)),
                  pl.BlockSpec((1, A_K), lambda i: (0, 0)), pl.BlockSpec((1, A_DV), lambda i: (0, 0))],
        out_specs=[pl.BlockSpec((rows, A_V), lambda i: (i, 0)),
                   pl.BlockSpec((rows, A_HEADS, A_DK, A_DV), lambda i: (i, 0, 0, 0))],
        out_shape=[jax.ShapeDtypeStruct((bs, A_V), BF16),
                   jax.ShapeDtypeStruct((bs, A_HEADS, A_DK, A_DV), F32)],
        compiler_params=_params("parallel"),
    )(hh, state, lb, a_norm)


POOL_HALO = 16


def _pool_project(m_groups, pw_ref, ps_ref, o_ref):
    for gi, m in enumerate(m_groups):
        sl = slice(gi * POOL_GW, (gi + 1) * POOL_GW)
        o_ref[:, sl] = (_dot(m.astype(BF16), pw_ref[gi]) * ps_ref[:, sl]).astype(o_ref.dtype)


def _pool_kernel(p_ref, pw_ref, ps_ref, o_ref, prev_scr, *, t):
    ti = pl.program_id(1)

    @pl.when(ti == 0)
    def _():
        prev_scr[...] = jnp.zeros_like(prev_scr)

    z = p_ref[...]
    zext = jnp.concatenate([prev_scr[...], z], axis=0)
    prev_scr[...] = z[t - POOL_HALO:, :]
    pos1 = (ti * t + 1 + lax.broadcasted_iota(jnp.int32, (t, 1), 0)).astype(F32)
    s = zext
    groups = []
    for gi, w in enumerate(POOL_WINDOWS):
        s = s[:, (POOL_GW if gi else 0):]
        s = s + pltpu.roll(s, w // 2, 0)
        sl = slice(gi * POOL_GW, (gi + 1) * POOL_GW)
        groups.append(s[POOL_HALO:, :POOL_GW] / jnp.minimum(pos1, float(w)) - z[:, sl])
    _pool_project(groups, pw_ref, ps_ref, o_ref)


def _pool_prompt(p, pool_w, pool_scale, batch, seq):
    t = _tile(seq, 512)
    nt = seq // t
    return pl.pallas_call(
        functools.partial(_pool_kernel, t=t), grid=(batch, nt),
        in_specs=[pl.BlockSpec((t, POOL_W), lambda b, i: (b * nt + i, 0)),
                  pl.BlockSpec(pool_w.shape, lambda b, i: (0, 0, 0)),
                  pl.BlockSpec((1, POOL_W), lambda b, i: (0, 0))],
        out_specs=pl.BlockSpec((t, POOL_W), lambda b, i: (b * nt + i, 0)),
        out_shape=jax.ShapeDtypeStruct((batch * seq, POOL_W), BF16),
        scratch_shapes=[pltpu.VMEM((POOL_HALO, POOL_W), F32)],
        compiler_params=_params("parallel", "arbitrary"),
    )(p, pool_w, pool_scale)


def _pool_step_kernel(p_ref, buf_ref, pw_ref, ps_ref, o_ref, nb_ref, *, past_len):
    z = p_ref[...]
    groups = []
    for gi, w in enumerate(POOL_WINDOWS):
        s = z[:, gi * POOL_GW:(gi + 1) * POOL_GW]
        for r in range(POOL_BUF - (w - 1), POOL_BUF):
            s = s + buf_ref[0, :, r * POOL_W + gi * POOL_GW:r * POOL_W + (gi + 1) * POOL_GW]
        groups.append(s / float(min(past_len + 1, w)) - z[:, gi * POOL_GW:(gi + 1) * POOL_GW])
    _pool_project(groups, pw_ref, ps_ref, o_ref)
    nb_ref[:, :(POOL_BUF - 1) * POOL_W] = buf_ref[0, :, POOL_W:]
    nb_ref[:, (POOL_BUF - 1) * POOL_W:] = z


def _pool_step(p, state_pool, layer, pool_w, pool_scale, past_len):
    bs = p.shape[0]
    flat = POOL_BUF * POOL_W
    pm, nb = pl.pallas_call(
        functools.partial(_pool_step_kernel, past_len=past_len), grid=(1,),
        in_specs=[pl.BlockSpec((bs, POOL_W), lambda i: (0, 0)),
                  pl.BlockSpec((1, bs, flat), lambda i: (layer, 0, 0)),
                  pl.BlockSpec(pool_w.shape, lambda i: (0, 0, 0)),
                  pl.BlockSpec((1, POOL_W), lambda i: (0, 0))],
        out_specs=[pl.BlockSpec((bs, POOL_W), lambda i: (0, 0)),
                   pl.BlockSpec((bs, flat), lambda i: (0, 0))],
        out_shape=[jax.ShapeDtypeStruct((bs, POOL_W), BF16),
                   jax.ShapeDtypeStruct((bs, flat), F32)],
        compiler_params=_params("arbitrary"),
    )(p, state_pool.reshape(state_pool.shape[0], bs, flat), pool_w, pool_scale)
    return pm, nb.reshape(bs, POOL_BUF, POOL_W)


MLA_IN = Q_LORA + KV_LORA + 2 * ROPE
Q_HEAD_COLS = NOPE + 2 * ROPE


def _rope(pair, cs):
    prod = pair * cs
    return prod[:, :ROPE] + prod[:, ROPE:]


def _mla_prep_kernel(h_ref, cs_ref, qn_ref, kn_ref, wuq_ref, wuk_ref, ckv_ref, kr_ref, kcat_ref, qcat_ref):
    cs = cs_ref[...]
    c_kv = _rms(h_ref[:, Q_LORA:Q_LORA + KV_LORA]) * kn_ref[...]
    k_rope = _rope(h_ref[:, Q_LORA + KV_LORA:], cs)
    ckv_ref[...] = c_kv
    kr_ref[...] = k_rope
    kcat_ref[:, :KV_LORA] = c_kv.astype(BF16)
    kcat_ref[:, KV_LORA:] = k_rope.astype(BF16)
    cq = (_rms(h_ref[:, :Q_LORA]) * qn_ref[...]).astype(BF16)
    qf = _dot(cq, wuq_ref[...])
    qscale = ATTN_SCALE * LOG2E
    for h in range(M_HEADS):
        base = h * Q_HEAD_COLS
        q_lat = _dot(qf[:, base:base + NOPE].astype(BF16), wuk_ref[h])
        q_rope = _rope(qf[:, base + NOPE:base + Q_HEAD_COLS], cs)
        qcat_ref[0, h, :, :KV_LORA] = (q_lat * qscale).astype(BF16)
        qcat_ref[0, h, :, KV_LORA:] = (q_rope * qscale).astype(BF16)


def _mla_prep(hm, cs, q_norm, kv_norm, wuq, wuk_t, batch, seq):
    t = _tile(seq, 512)
    nt = seq // t
    n = batch * seq
    row = lambda w: pl.BlockSpec((t, w), lambda b, i: (b * nt + i, 0))
    const = lambda shape: pl.BlockSpec(shape, lambda b, i: (0,) * len(shape))
    return pl.pallas_call(
        _mla_prep_kernel, grid=(batch, nt),
        in_specs=[row(MLA_IN), pl.BlockSpec((t, 2 * ROPE), lambda b, i: (i, 0)),
                  const((1, Q_LORA)), const((1, KV_LORA)), const(wuq.shape), const(wuk_t.shape)],
        out_specs=[row(KV_LORA), row(ROPE), row(QK_CAT),
                   pl.BlockSpec((1, M_HEADS, t, QK_CAT), lambda b, i: (b, 0, i, 0))],
        out_shape=[jax.ShapeDtypeStruct((n, KV_LORA), F32), jax.ShapeDtypeStruct((n, ROPE), F32),
                   jax.ShapeDtypeStruct((n, QK_CAT), BF16),
                   jax.ShapeDtypeStruct((batch, M_HEADS, seq, QK_CAT), BF16)],
        compiler_params=_params("parallel", "parallel"),
    )(hm, cs, q_norm, kv_norm, wuq, wuk_t)


LANES = 128
FLASH_SUB = 256


def _lane_tile(x, n):
    return jnp.concatenate([x] * n, axis=1)


def _flash_kernel(q_ref, k_ref, wuv_ref, o_ref, m_scr, l_scr, acc_scr, *, bq, sub):
    qi = pl.program_id(1)
    m_scr[...] = jnp.full_like(m_scr, -jnp.inf)
    l_scr[...] = jnp.zeros_like(l_scr)
    acc_scr[...] = jnp.zeros_like(acc_scr)
    lane_tiles = bq // LANES

    def step(j, masked):
        kblk = k_ref[0, pl.ds(pl.multiple_of(j * bq, bq), bq), :]
        vblk = kblk[:, :KV_LORA]
        chains = [(h, u) for h in range(M_HEADS) for u in range(bq // sub)]

        def scores(h, u):
            q = q_ref[0, h, u * sub:(u + 1) * sub, :]
            return lax.dot_general(q, kblk, _NT, preferred_element_type=F32)

        s_next = scores(*chains[0])
        for ci, (h, u) in enumerate(chains):
            s = s_next
            if ci + 1 < len(chains):
                s_next = scores(*chains[ci + 1])
            rows = pl.ds(h * bq + u * sub, sub)
            if masked:
                tq = u * sub + lax.broadcasted_iota(jnp.int32, (sub, bq), 0)
                tk = lax.broadcasted_iota(jnp.int32, (sub, bq), 1)
                s = jnp.where(tk <= tq, s, -jnp.inf)
            m_prev = m_scr[rows, :]
            m_new = jnp.maximum(m_prev, jnp.max(s, axis=1, keepdims=True))
            alpha = jnp.exp2(m_prev - m_new)
            p = jnp.exp2(s - _lane_tile(m_new, lane_tiles))
            l_scr[rows, :] = alpha * l_scr[rows, :] + jnp.sum(p, axis=1, keepdims=True)
            acc_scr[rows, :] = (_lane_tile(alpha, KV_LORA // LANES) * acc_scr[rows, :]
                                + _dot(p.astype(BF16), vblk))
            m_scr[rows, :] = m_new

    def body(j, carry):
        step(j, False)
        return carry

    lax.fori_loop(0, qi, body, 0)
    step(qi, True)
    for h in range(M_HEADS):
        rows = pl.ds(h * bq, bq)
        inv_l = _lane_tile(1.0 / l_scr[rows, :], KV_LORA // LANES)
        o_ref[:, h * M_VD:(h + 1) * M_VD] = _dot((acc_scr[rows, :] * inv_l).astype(BF16),
                                                 wuv_ref[h]).astype(o_ref.dtype)


def _flash_prompt(qcat, kcat, wuv, batch, seq):
    bq = _tile(seq, 512)
    nq = seq // bq
    rows = M_HEADS * bq
    return pl.pallas_call(
        functools.partial(_flash_kernel, bq=bq, sub=min(FLASH_SUB, bq)), grid=(batch, nq),
        in_specs=[pl.BlockSpec((1, M_HEADS, bq, QK_CAT), lambda b, i: (b, 0, i, 0)),
                  pl.BlockSpec((1, seq, QK_CAT), lambda b, i: (b, 0, 0)),
                  pl.BlockSpec(wuv.shape, lambda b, i: (0, 0, 0))],
        out_specs=pl.BlockSpec((bq, M_V), lambda b, i: (b * nq + i, 0)),
        out_shape=jax.ShapeDtypeStruct((batch * seq, M_V), BF16),
        scratch_shapes=[pltpu.VMEM((rows, LANES), F32), pltpu.VMEM((rows, LANES), F32),
                        pltpu.VMEM((rows, KV_LORA), F32)],
        compiler_params=_params("parallel", "arbitrary"),
    )(qcat, kcat.reshape(batch, seq, QK_CAT), wuv)


Q_ROWS = 8
PAGES_PER_STEP = 32
DECODE_SUB_KEYS = 1024


def _decode_kernel(pt_ref, q_ref, cnew_ref, rnew_ref, wuv_ref, ckv_hbm, kr_hbm, o_ref,
                   kbuf, rbuf, sem, m_scr, l_scr, acc_scr, *, layer, ppc, page):
    b = pl.program_id(0)
    c = pl.program_id(1)
    nb = pl.num_programs(0)
    nch = pl.num_programs(1)
    step = b * nch + c
    slot = step % 2

    def copies(bb, cc, sl, j):
        pg = pt_ref[bb, cc * ppc + j]
        keys = pl.ds(j * page, page)
        return (pltpu.make_async_copy(ckv_hbm.at[layer, pg], kbuf.at[sl, keys], sem.at[0, sl]),
                pltpu.make_async_copy(kr_hbm.at[layer, pg], rbuf.at[sl, :, keys], sem.at[1, sl]))

    def issue(bb, cc, sl):
        for j in range(ppc):
            for cp in copies(bb, cc, sl, j):
                cp.start()

    @pl.when(step == 0)
    def _():
        issue(b, c, slot)

    @pl.when(step + 1 < nb * nch)
    def _():
        nxt = step + 1
        issue(nxt // nch, nxt % nch, 1 - slot)

    q = q_ref[0]
    q_lat = q[:, :KV_LORA]
    q_rope = q[:, KV_LORA:]

    @pl.when(c == 0)
    def _():
        c_new = cnew_ref[0]
        m_scr[...] = (jnp.sum(q_lat.astype(F32) * c_new, axis=1, keepdims=True)
                      + jnp.sum(q_rope.astype(F32) * rnew_ref[0], axis=1, keepdims=True))
        l_scr[...] = jnp.ones_like(l_scr)
        acc_scr[...] = jnp.broadcast_to(c_new, acc_scr.shape)

    for j in range(ppc):
        for cp in copies(b, c, slot, j):
            cp.wait()

    sub = min(DECODE_SUB_KEYS, ppc * page)
    parts = [(m_scr[...], l_scr[...], acc_scr[...])]
    for u in range(ppc * page // sub):
        keys = pl.ds(u * sub, sub)
        kc = kbuf[slot, keys, :].astype(BF16)
        kr_t = rbuf[slot, :, keys].astype(BF16)
        s = lax.dot_general(q_lat, kc, _NT, preferred_element_type=F32) + _dot(q_rope, kr_t)
        m_u = jnp.max(s, axis=1, keepdims=True)
        p = jnp.exp2(s - m_u)
        parts.append((m_u, jnp.sum(p, axis=1, keepdims=True), _dot(p.astype(BF16), kc)))
    m_new = functools.reduce(jnp.maximum, [m for m, _, _ in parts])
    weights = [jnp.exp2(m - m_new) for m, _, _ in parts]
    m_scr[...] = m_new
    l_scr[...] = sum(w * l for w, (_, l, _) in zip(weights, parts))
    acc_scr[...] = sum(w * a for w, (_, _, a) in zip(weights, parts))

    @pl.when(c == nch - 1)
    def _():
        o_lat = (acc_scr[...] / l_scr[...]).astype(BF16)
        for h in range(M_HEADS):
            o_ref[0, :, h * M_VD:(h + 1) * M_VD] = _dot(o_lat, wuv_ref[h])[h:h + 1].astype(o_ref.dtype)


def _mla_decode(page_table, q8, c_new, r_new, wuv, cache_ckv, cache_krope_t, layer):
    bs, n_pages = page_table.shape
    page = cache_ckv.shape[2]
    ppc = PAGES_PER_STEP if n_pages % PAGES_PER_STEP == 0 else n_pages
    nch = n_pages // ppc
    grid_spec = pltpu.PrefetchScalarGridSpec(
        num_scalar_prefetch=1, grid=(bs, nch),
        in_specs=[pl.BlockSpec((1, Q_ROWS, QK_CAT), lambda b, c, pt: (b, 0, 0)),
                  pl.BlockSpec((1, 1, KV_LORA), lambda b, c, pt: (b, 0, 0)),
                  pl.BlockSpec((1, 1, ROPE), lambda b, c, pt: (b, 0, 0)),
                  pl.BlockSpec(wuv.shape, lambda b, c, pt: (0, 0, 0)),
                  pl.BlockSpec(memory_space=pl.ANY), pl.BlockSpec(memory_space=pl.ANY)],
        out_specs=pl.BlockSpec((1, 1, M_V), lambda b, c, pt: (b, 0, 0)),
        scratch_shapes=[pltpu.VMEM((2, ppc * page, KV_LORA), F32), pltpu.VMEM((2, ROPE, ppc * page), F32),
                        pltpu.SemaphoreType.DMA((2, 2)),
                        pltpu.VMEM((Q_ROWS, 1), F32), pltpu.VMEM((Q_ROWS, 1), F32),
                        pltpu.VMEM((Q_ROWS, KV_LORA), F32)])
    return pl.pallas_call(
        functools.partial(_decode_kernel, layer=layer, ppc=ppc, page=page),
        grid_spec=grid_spec, out_shape=jax.ShapeDtypeStruct((bs, 1, M_V), BF16),
        compiler_params=_params("arbitrary", "arbitrary"),
    )(page_table, q8, c_new, r_new, wuv, cache_ckv, cache_krope_t)


def _merge_kernel(og_ref, pm_ref, ov_ref, gt_ref, x_ref, wa_ref, wp_ref, wm_ref, wo_ref, g_ref, b_ref,
                  o_ref, ob_ref, *, alpha):
    d = D_MODEL
    merged = (gt_ref[:, :d].astype(F32) * _dot(og_ref[...], wa_ref[...])
              + gt_ref[:, d:2 * d].astype(F32) * _dot(pm_ref[...], wp_ref[...])
              + gt_ref[:, 2 * d:].astype(F32) * _dot(ov_ref[...], wm_ref[...]))
    y = alpha * x_ref[...] + _dot(merged.astype(BF16), wo_ref[...])
    out = _layernorm(y, g_ref[...], b_ref[...])
    o_ref[...] = out
    ob_ref[...] = out.astype(BF16)


def _merge(og, pm, ov, gates, x, wa, wp, wm, wo, g, b, alpha):
    n = x.shape[0]
    t = _tile(n, 512)
    row = lambda w: pl.BlockSpec((t, w), lambda i: (i, 0))
    const = lambda a: pl.BlockSpec(a.shape, lambda i: (0,) * a.ndim)
    return pl.pallas_call(
        functools.partial(_merge_kernel, alpha=alpha), grid=(n // t,),
        in_specs=[row(A_V), row(POOL_W), row(M_V), row(N_BRANCH * D_MODEL), row(D_MODEL),
                  const(wa), const(wp), const(wm), const(wo), const(g), const(b)],
        out_specs=[row(D_MODEL), row(D_MODEL)],
        out_shape=[jax.ShapeDtypeStruct((n, D_MODEL), F32), jax.ShapeDtypeStruct((n, D_MODEL), BF16)],
        compiler_params=_params("parallel"),
    )(og, pm, ov, gates, x, wa, wp, wm, wo, g, b)


FF_BLOCK = D_FF // 2


def _swiglu_block(xb, w1, w3, w2):
    a = _dot(xb, w1)
    return _dot((a * _sigmoid(a) * _dot(xb, w3)).astype(BF16), w2)


def _ffn_kernel(xb_ref, x_ref, w1_ref, w3_ref, w2_ref, g_ref, b_ref, o_ref, ob_ref, acc_scr, *, alpha):
    j = pl.program_id(1)

    @pl.when(j == 0)
    def _():
        acc_scr[...] = jnp.zeros_like(acc_scr)

    acc_scr[...] += _swiglu_block(xb_ref[...], w1_ref[...], w3_ref[...], w2_ref[...])

    @pl.when(j == pl.num_programs(1) - 1)
    def _():
        out = _layernorm(alpha * x_ref[...] + acc_scr[...], g_ref[...], b_ref[...])
        o_ref[...] = out
        ob_ref[...] = out.astype(BF16)


def _ffn(xb, x, w1, w3, w2, g, b, alpha):
    n = x.shape[0]
    t = _tile(n, 512)
    nf = D_FF // FF_BLOCK
    row = lambda: pl.BlockSpec((t, D_MODEL), lambda i, j: (i, 0))
    vec = lambda: pl.BlockSpec((1, D_MODEL), lambda i, j: (0, 0))
    return pl.pallas_call(
        functools.partial(_ffn_kernel, alpha=alpha), grid=(n // t, nf),
        in_specs=[row(), row(), pl.BlockSpec((D_MODEL, FF_BLOCK), lambda i, j: (0, j)),
                  pl.BlockSpec((D_MODEL, FF_BLOCK), lambda i, j: (0, j)),
                  pl.BlockSpec((FF_BLOCK, D_MODEL), lambda i, j: (j, 0)), vec(), vec()],
        out_specs=[row(), row()],
        out_shape=[jax.ShapeDtypeStruct((n, D_MODEL), F32), jax.ShapeDtypeStruct((n, D_MODEL), BF16)],
        scratch_shapes=[pltpu.VMEM((t, D_MODEL), F32)],
        compiler_params=_params("parallel", "arbitrary"),
    )(xb, x, w1, w3, w2, g, b)


TOP_K = 2


def _router_kernel(x_ref, rt_ref, idx_ref, w_ref):
    logits = lax.dot_general(rt_ref[...], x_ref[...], _NT, precision=lax.Precision.HIGHEST,
                             preferred_element_type=F32)
    eid = lax.broadcasted_iota(jnp.int32, logits.shape, 0)
    m1 = jnp.max(logits, axis=0, keepdims=True)
    i1 = jnp.min(jnp.where(logits == m1, eid, N_EXPERTS), axis=0, keepdims=True)
    rest = jnp.where(eid == i1, -jnp.inf, logits)
    m2 = jnp.max(rest, axis=0, keepdims=True)
    i2 = jnp.min(jnp.where(rest == m2, eid, N_EXPERTS), axis=0, keepdims=True)
    e2 = jnp.exp(m2 - m1)
    idx_ref[...] = jnp.concatenate([i1, i2], axis=0)
    w_ref[...] = jnp.concatenate([1.0 / (1.0 + e2), e2 / (1.0 + e2)], axis=0)


def _router(x, router_t):
    n = x.shape[0]
    t = _tile(n, 1024)
    return pl.pallas_call(
        _router_kernel, grid=(n // t,),
        in_specs=[pl.BlockSpec((t, D_MODEL), lambda i: (i, 0)),
                  pl.BlockSpec((N_EXPERTS, D_MODEL), lambda i: (0, 0))],
        out_specs=[pl.BlockSpec((TOP_K, t), lambda i: (0, i)), pl.BlockSpec((TOP_K, t), lambda i: (0, i))],
        out_shape=[jax.ShapeDtypeStruct((TOP_K, n), jnp.int32), jax.ShapeDtypeStruct((TOP_K, n), F32)],
        compiler_params=_params("parallel"),
    )(x, router_t)


MOE_ROWS = 256
COMBINE_ROWS = 256


def _row_copy(src_hbm, row, dst_buf, slot, i, sem):
    return pltpu.make_async_copy(src_hbm.at[pl.ds(row, 1)], dst_buf.at[slot, pl.ds(i, 1)], sem.at[slot])


def _gather_rows(step, n_steps, cur_ref, nxt_ref, src_hbm, buf, sem, n_rows):
    slot = step % 2

    @pl.when(step == 0)
    def _():
        for i in range(n_rows):
            _row_copy(src_hbm, cur_ref[0, 0, i], buf, 0, i, sem).start()

    for i in range(n_rows):
        _row_copy(src_hbm, nxt_ref[0, 0, i], buf, 1 - slot, i, sem).start()
    for i in range(n_rows):
        _row_copy(src_hbm, 0, buf, slot, i, sem).wait()

    @pl.when(step == n_steps - 1)
    def _():
        for i in range(n_rows):
            _row_copy(src_hbm, 0, buf, 1 - slot, i, sem).wait()

    return slot


def _experts_kernel(be_ref, cur_ref, nxt_ref, x_hbm, w1_ref, w3_ref, w2_ref, y_ref, xbuf, sem, *, rb):
    del be_ref
    slot = _gather_rows(pl.program_id(0), pl.num_programs(0), cur_ref, nxt_ref, x_hbm, xbuf, sem, rb)
    y_ref[...] = _swiglu_block(xbuf[slot].astype(BF16), w1_ref[0], w3_ref[0], w2_ref[0])


def _combine_kernel(cur_ref, nxt_ref, x_ref, w_ref, y_hbm, g_ref, b_ref, o_ref, ob_ref, ybuf, sem, *, tt, alpha):
    slot = _gather_rows(pl.program_id(0), pl.num_programs(0), cur_ref, nxt_ref, y_hbm, ybuf, sem, TOP_K * tt)
    w = w_ref[...]
    ffn = w[:, 0:1] * ybuf[slot, :tt, :] + w[:, 1:2] * ybuf[slot, tt:, :]
    out = _layernorm(alpha * x_ref[...] + ffn, g_ref[...], b_ref[...])
    o_ref[...] = out
    ob_ref[...] = out.astype(BF16)


def _moe_plan(idx_t, rb):
    n = idx_t.shape[1]
    n_assign = TOP_K * n
    p_total = n_assign + N_EXPERTS * rb
    ex = idx_t.reshape(n_assign)
    tok = jnp.tile(jnp.arange(n, dtype=jnp.int32), TOP_K)
    order = jnp.argsort(ex, stable=True)
    ex_sorted = ex[order]
    counts = jnp.sum((ex[None, :] == jnp.arange(N_EXPERTS, dtype=jnp.int32)[:, None]).astype(jnp.int32), axis=1)
    padded = ((counts + rb - 1) // rb) * rb
    group_end = jnp.cumsum(padded)
    group_start = group_end - padded
    first = jnp.cumsum(counts) - counts
    dest = group_start[ex_sorted] + jnp.arange(n_assign, dtype=jnp.int32) - first[ex_sorted]
    row_token = jnp.zeros((p_total,), jnp.int32).at[dest].set(tok[order])
    position = jnp.zeros((n_assign,), jnp.int32).at[order].set(dest)
    block_start = jnp.arange(p_total // rb, dtype=jnp.int32) * rb
    block_expert = jnp.minimum(jnp.searchsorted(group_end, block_start, side='right'), N_EXPERTS - 1)
    return row_token, block_expert.astype(jnp.int32), position.reshape(TOP_K, n)


def _moe(x, idx_t, w_t, w1, w3, w2, g, b, alpha):
    n = x.shape[0]
    rb = min(MOE_ROWS, n)
    row_token, block_expert, position = _moe_plan(idx_t, rb)
    nb = row_token.shape[0] // rb
    rows3 = row_token.reshape(nb, 1, rb)
    smem = lambda width, nxt, last: pl.BlockSpec(
        (1, 1, width), lambda i, *_: (jnp.minimum(i + nxt, last), 0, 0), memory_space=pltpu.SMEM)
    weight = lambda shape: pl.BlockSpec((1,) + shape, lambda i, be: (be[i], 0, 0))
    y = pl.pallas_call(
        functools.partial(_experts_kernel, rb=rb),
        grid_spec=pltpu.PrefetchScalarGridSpec(
            num_scalar_prefetch=1, grid=(nb,),
            in_specs=[smem(rb, 0, nb - 1), smem(rb, 1, nb - 1), pl.BlockSpec(memory_space=pl.ANY),
                      weight((D_MODEL, D_FF)), weight((D_MODEL, D_FF)), weight((D_FF, D_MODEL))],
            out_specs=pl.BlockSpec((rb, D_MODEL), lambda i, be: (i, 0)),
            scratch_shapes=[pltpu.VMEM((2, rb, D_MODEL), F32), pltpu.SemaphoreType.DMA((2,))]),
        out_shape=jax.ShapeDtypeStruct((nb * rb, D_MODEL), F32),
        compiler_params=_params("arbitrary"),
    )(block_expert, rows3, rows3, x, w1, w3, w2)

    tt = min(COMBINE_ROWS, n)
    nt = n // tt
    pos3 = position.reshape(TOP_K, nt, tt).transpose(1, 0, 2).reshape(nt, 1, TOP_K * tt)
    row = lambda width: pl.BlockSpec((tt, width), lambda i: (i, 0))
    vec = lambda: pl.BlockSpec((1, D_MODEL), lambda i: (0, 0))
    return pl.pallas_call(
        functools.partial(_combine_kernel, tt=tt, alpha=alpha), grid=(nt,),
        in_specs=[smem(TOP_K * tt, 0, nt - 1), smem(TOP_K * tt, 1, nt - 1), row(D_MODEL), row(TOP_K),
                  pl.BlockSpec(memory_space=pl.ANY), vec(), vec()],
        out_specs=[row(D_MODEL), row(D_MODEL)],
        out_shape=[jax.ShapeDtypeStruct((n, D_MODEL), F32), jax.ShapeDtypeStruct((n, D_MODEL), BF16)],
        scratch_shapes=[pltpu.VMEM((2, TOP_K * tt, D_MODEL), F32), pltpu.SemaphoreType.DMA((2,))],
        compiler_params=_params("arbitrary"),
    )(pos3, pos3, x, w_t.T, y, g, b)


def _swap_halves_cols(w):
    half = w.shape[-1] // 2
    return jnp.concatenate([w[..., half:], w[..., :half]], axis=-1)


def _rope_table(pos):
    half = ROPE // 2
    inv = ROPE_BASE ** (-jnp.arange(half, dtype=F32) / half)
    ang = pos.astype(F32)[:, None] * inv[None, :]
    c, s = jnp.cos(ang), jnp.sin(ang)
    return jnp.concatenate([c, c, -s, s], axis=-1)


def _layer_weights(l, w_in, b_gate, lb, hgrn_norm, w_hgrn_out, pool_w, pool_scale, w_pool_out,
                   mla_q_norm, w_uq, mla_kv_norm, w_uk, w_uv, w_mla_out, w_o):
    offs = np.cumsum((0, A_K, A_K, A_V, A_V, POOL_W, Q_LORA, KV_LORA, ROPE))
    wi = w_in[l]
    w_kr = wi[:, offs[7]:offs[8]]
    uq = w_uq[l].reshape(Q_LORA, M_HEADS, NOPE + ROPE)
    uq = jnp.concatenate([uq, _swap_halves_cols(uq[..., NOPE:])], axis=-1)
    return dict(
        w_hgrn=wi[:, :offs[4]].astype(BF16),
        w_pool=wi[:, offs[4]:offs[5]].astype(BF16),
        w_mla=jnp.concatenate([wi[:, offs[5]:offs[8]], _swap_halves_cols(w_kr)], axis=1).astype(BF16),
        w_gate=wi[:, offs[8]:].astype(BF16),
        b_gate=b_gate[l].reshape(1, N_BRANCH * D_MODEL),
        lb=lb[l].reshape(1, A_K),
        a_norm=hgrn_norm[l].reshape(1, A_DV),
        w_a_out=w_hgrn_out[l].astype(BF16),
        pool_w=pool_w[l].astype(BF16),
        pool_scale=pool_scale[l].reshape(1, POOL_W),
        w_p_out=w_pool_out[l].astype(BF16),
        q_norm=mla_q_norm[l].reshape(1, Q_LORA),
        kv_norm=mla_kv_norm[l].reshape(1, KV_LORA),
        w_uq=uq.reshape(Q_LORA, M_HEADS * Q_HEAD_COLS).astype(BF16),
        w_uk_t=jnp.transpose(w_uk[l], (1, 2, 0)).astype(BF16),
        w_uv=jnp.transpose(w_uv[l], (1, 0, 2)).astype(BF16),
        w_m_out=w_mla_out[l].astype(BF16),
        w_o=w_o[l].astype(BF16),
    )


def _in_projections(xb, lw):
    hh = _matmul(xb, lw['w_hgrn'], F32, 1024)
    hp = _matmul(xb, lw['w_pool'], F32, POOL_W)
    hm = _matmul(xb, lw['w_mla'], F32, MLA_IN)
    gates = _matmul(xb, lw['w_gate'], BF16, 1024, bias=lw['b_gate'])
    return hh, hp, hm, gates


def kernel(x_prompt, x_sample, cache_ckv, cache_krope, state_hgrn, state_pool, page_table, w_in, b_gate, hgrn_lb_logits, hgrn_norm, w_hgrn_out, pool_w, pool_scale, w_pool_out, mla_q_norm, w_uq, mla_kv_norm, w_uk, w_uv, w_mla_out, w_o, ln1_g, ln1_b, ln2_g, ln2_b, ffn_w1, ffn_w3, ffn_w2, moe_router, moe_w1, moe_w3, moe_w2):
    bp, lp, _ = x_prompt.shape
    bs, ls, _ = x_sample.shape
    assert ls == 1, "the decode path handles one new token per sequence"
    depth = w_in.shape[0]
    past_len = page_table.shape[1] * cache_ckv.shape[2]
    alpha = (2 * depth) ** 0.25
    npr = bp * lp

    lb_cum = jnp.cumsum(jax.nn.softmax(hgrn_lb_logits.astype(F32), axis=0), axis=0)
    lb = lb_cum - lb_cum[0:1]
    cs_p = _rope_table(jnp.arange(lp, dtype=jnp.int32))
    cs_s = _rope_table(jnp.full((bs,), past_len, jnp.int32))
    vec = lambda a: a.reshape(1, D_MODEL)
    cache_krope_t = jnp.swapaxes(cache_krope, 2, 3)

    xp = x_prompt.reshape(npr, D_MODEL)
    xs = x_sample.reshape(bs, D_MODEL)
    xpb, xsb = xp.astype(BF16), xs.astype(BF16)
    outs = {k: [] for k in ('ckv_p', 'kr_p', 'ckv_s', 'kr_s', 'sh_p', 'sh_s', 'pb_p', 'pb_s')}
    for l in range(depth):
        lw = _layer_weights(l, w_in, b_gate, lb, hgrn_norm, w_hgrn_out, pool_w, pool_scale, w_pool_out,
                            mla_q_norm, w_uq, mla_kv_norm, w_uk, w_uv, w_mla_out, w_o)
        hh, hp, hm, gates = _in_projections(xpb, lw)
        og, s_p = _hgrn_prompt(hh, lw['lb'], lw['a_norm'], bp, lp)
        pm = _pool_prompt(hp, lw['pool_w'], lw['pool_scale'], bp, lp)
        ckv, kr, kcat, qcat = _mla_prep(hm, cs_p, lw['q_norm'], lw['kv_norm'], lw['w_uq'], lw['w_uk_t'], bp, lp)
        ov = _flash_prompt(qcat, kcat, lw['w_uv'], bp, lp)
        xp1, xp1b = _merge(og, pm, ov, gates, xp, lw['w_a_out'], lw['w_p_out'], lw['w_m_out'], lw['w_o'],
                           vec(ln1_g[l]), vec(ln1_b[l]), alpha)
        outs['ckv_p'].append(ckv.reshape(bp, lp, KV_LORA))
        outs['kr_p'].append(kr.reshape(bp, lp, ROPE))
        outs['sh_p'].append(s_p)
        outs['pb_p'].append(hp.reshape(bp, lp, POOL_W)[:, lp - POOL_BUF:])
        hh, hp, hm, gates = _in_projections(xsb, lw)
        og, s_s = _hgrn_step(hh, state_hgrn, l, lw['lb'], lw['a_norm'])
        pm, pb_s = _pool_step(hp, state_pool, l, lw['pool_w'], lw['pool_scale'], past_len)
        ckv, kr, _, qcat = _mla_prep(hm, cs_s, lw['q_norm'], lw['kv_norm'], lw['w_uq'], lw['w_uk_t'], 1, bs)
        q8 = jnp.pad(jnp.transpose(qcat[0], (1, 0, 2)), ((0, 0), (0, Q_ROWS - M_HEADS), (0, 0)))
        ckv = ckv.reshape(bs, 1, KV_LORA)
        kr = kr.reshape(bs, 1, ROPE)
        ov = _mla_decode(page_table, q8, ckv, kr, lw['w_uv'], cache_ckv, cache_krope_t, l).reshape(bs, M_V)
        xs1, xs1b = _merge(og, pm, ov, gates, xs, lw['w_a_out'], lw['w_p_out'], lw['w_m_out'], lw['w_o'],
                           vec(ln1_g[l]), vec(ln1_b[l]), alpha)
        outs['ckv_s'].append(ckv)
        outs['kr_s'].append(kr)
        outs['sh_s'].append(s_s)
        outs['pb_s'].append(pb_s)
        j = l // 2
        if l % 2 == 0:
            w1, w3, w2 = ffn_w1[j].astype(BF16), ffn_w3[j].astype(BF16), ffn_w2[j].astype(BF16)
            xp, xpb = _ffn(xp1b, xp1, w1, w3, w2, vec(ln2_g[l]), vec(ln2_b[l]), alpha)
            xs, xsb = _ffn(xs1b, xs1, w1, w3, w2, vec(ln2_g[l]), vec(ln2_b[l]), alpha)
        else:
            w1, w3, w2 = moe_w1[j].astype(BF16), moe_w3[j].astype(BF16), moe_w2[j].astype(BF16)
            router_t = moe_router[j].T
            xp, xpb = _moe(xp1, *_router(xp1, router_t), w1, w3, w2, vec(ln2_g[l]), vec(ln2_b[l]), alpha)
            xs, xsb = _moe(xs1, *_router(xs1, router_t), w1, w3, w2, vec(ln2_g[l]), vec(ln2_b[l]), alpha)

    st = lambda k: jnp.stack(outs[k])
    return (xp.reshape(bp, lp, D_MODEL), xs.reshape(bs, ls, D_MODEL), st('ckv_p'), st('kr_p'), st('ckv_s'),
            st('kr_s'), st('sh_p'), st('sh_s'), st('pb_p'), st('pb_s'))
```

```python
import functools
import math

import numpy as np
import jax
import jax.numpy as jnp
from jax import lax
from jax.experimental import pallas as pl
from jax.experimental.pallas import tpu as pltpu

D_MODEL = 1024
A_HEADS = 4
A_DK = 128
A_DV = 128
A_K = A_HEADS * A_DK
A_V = A_HEADS * A_DV
POOL_WINDOWS = (2, 4, 8, 16)
POOL_GW = 128
POOL_W = len(POOL_WINDOWS) * POOL_GW
POOL_BUF = max(POOL_WINDOWS) - 1
M_HEADS = 4
Q_LORA = 384
KV_LORA = 256
NOPE = 128
ROPE = 64
M_VD = 128
M_V = M_HEADS * M_VD
QK_CAT = KV_LORA + ROPE
ROPE_BASE = 10000.0
ATTN_SCALE = (NOPE + ROPE) ** -0.5
N_BRANCH = 3
D_FF = 2816
N_EXPERTS = 8
LN_EPS = 1e-5
RMS_EPS = 1e-6
LOG2E = 1.4426950408889634

BF16 = jnp.bfloat16
F32 = jnp.float32
VMEM_LIMIT = 56 * 2 ** 20

_NT = (((1,), (1,)), ((), ()))
_TN = (((0,), (0,)), ((), ()))


def _params(*sem):
    return pltpu.CompilerParams(dimension_semantics=sem, vmem_limit_bytes=VMEM_LIMIT)


def _tile(n, pref):
    return pref if n % pref == 0 else n


def _dot(a, b):
    return jnp.dot(a, b, preferred_element_type=F32)


def _sigmoid(x):
    return 1.0 / (1.0 + jnp.exp(-x))


def _layernorm(y, g, b):
    mu = jnp.mean(y, axis=-1, keepdims=True)
    d = y - mu
    var = jnp.mean(d * d, axis=-1, keepdims=True)
    return d * lax.rsqrt(var + LN_EPS) * g + b


def _rms(x):
    return x * lax.rsqrt(jnp.mean(x * x, axis=-1, keepdims=True) + RMS_EPS)


def _mm_kernel(x_ref, w_ref, o_ref):
    o_ref[...] = _dot(x_ref[...], w_ref[...]).astype(o_ref.dtype)


def _mm_gate_kernel(x_ref, w_ref, b_ref, o_ref):
    o_ref[...] = _sigmoid(_dot(x_ref[...], w_ref[...]) + b_ref[...]).astype(o_ref.dtype)


def _matmul(x, w, out_dtype, tn, bias=None):
    n, k = x.shape
    m = w.shape[1]
    tm = _tile(n, 1024)
    in_specs = [pl.BlockSpec((tm, k), lambda i, j: (i, 0)), pl.BlockSpec((k, tn), lambda i, j: (0, j))]
    args = [x, w]
    body = _mm_kernel
    if bias is not None:
        in_specs.append(pl.BlockSpec((1, tn), lambda i, j: (0, j)))
        args.append(bias)
        body = _mm_gate_kernel
    return pl.pallas_call(
        body, grid=(n // tm, m // tn), in_specs=in_specs,
        out_specs=pl.BlockSpec((tm, tn), lambda i, j: (i, j)),
        out_shape=jax.ShapeDtypeStruct((n, m), out_dtype),
        compiler_params=_params("parallel", "arbitrary"))(*args)


HGRN_CHUNK = 128


def _hgrn_tables(c):
    nlev = int(math.log2(c))
    t = np.arange(c)
    u = t[None, :]
    mats = np.zeros((nlev + 2, c, c), np.float32)
    lvl = -np.ones((c, c), np.int32)
    for l in range(nlev):
        blk = c >> l
        s0 = (t // blk) * blk
        mid = s0 + blk // 2
        qrole = t >= mid
        row_q = (u > mid[:, None]) & (u <= t[:, None])
        row_k = (u > t[:, None]) & (u <= mid[:, None])
        mats[l] = np.where(qrole[:, None], row_q, row_k)
        pair = (s0[:, None] == s0[None, :]) & qrole[:, None] & (~qrole)[None, :]
        lvl[pair] = l
    lvl[t, t] = nlev
    mats[nlev] = u <= t[:, None]
    mats[nlev + 1] = u > t[:, None]
    return mats.reshape((nlev + 2) * c, c), lvl, nlev


def _hgrn_gates(f, lb):
    log_sig = jnp.minimum(f, 0.0) - jnp.log1p(jnp.exp(-jnp.abs(f)))
    b = jnp.log1p(-lb) + log_sig
    a = jnp.log(jnp.maximum(lb, 1e-37))
    both = jnp.maximum(a, b) + jnp.log1p(jnp.exp(-jnp.abs(a - b)))
    logf = jnp.where(lb > 0.0, both, b)
    kk = (1.0 - lb) * _sigmoid(-f)
    return logf, kk


def _hgrn_kernel(q_ref, f_ref, i_ref, g_ref, lb_ref, an_ref, mall_ref, lvl_ref, og_ref, s_ref, st_scr,
                 *, c, nlev):
    ci = pl.program_id(1)

    @pl.when(ci == 0)
    def _():
        st_scr[...] = jnp.zeros_like(st_scr)

    lvl = lvl_ref[...]
    an = an_ref[...]
    heads = range(A_HEADS)
    hsl = [slice(h * A_DK, (h + 1) * A_DK) for h in heads]
    q = q_ref[...]
    qh = q * _sigmoid(q)
    logf, kk = _hgrn_gates(f_ref[...], lb_ref[...])
    hi = logf.astype(BF16)
    mid = (logf - hi.astype(F32)).astype(BF16)
    d2 = _dot(mall_ref[...], jnp.concatenate([hi, mid], axis=1))
    e = jnp.exp(d2[:, :A_K] + d2[:, A_K:])
    a = [jnp.zeros((c, c), F32) for _ in heads]
    for l in range(nlev + 1):
        for h in heads:
            if l < nlev:
                el = e[l * c:(l + 1) * c, hsl[h]]
                x, y = qh[:, hsl[h]] * el, kk[:, hsl[h]] * el
            else:
                x, y = qh[:, hsl[h]], kk[:, hsl[h]]
            p = lax.dot_general(x.astype(BF16), y.astype(BF16), _NT, preferred_element_type=F32)
            a[h] = jnp.where(lvl == l, p, a[h])
    eb = e[nlev * c:(nlev + 1) * c]
    ke = (kk * e[(nlev + 1) * c:]).astype(BF16)
    qe = (qh * eb).astype(BF16)
    v = i_ref[...].astype(BF16)
    gate = _sigmoid(g_ref[...])
    for h in heads:
        st = st_scr[h]
        o = _dot(a[h].astype(BF16), v[:, hsl[h]]) + lax.dot_general(qe[:, hsl[h]], st.astype(BF16), _NT,
                                                                    preferred_element_type=F32)
        st_new = st * eb[c - 1:c, hsl[h]] + lax.dot_general(v[:, hsl[h]], ke[:, hsl[h]], _TN,
                                                             preferred_element_type=F32)
        st_scr[h] = st_new
        og_ref[:, hsl[h]] = (_rms(o) * an * gate[:, hsl[h]]).astype(og_ref.dtype)

        @pl.when(ci == pl.num_programs(1) - 1)
        def _():
            s_ref[0, h] = st_new.T


def _hgrn_prompt(hh, lb, a_norm, batch, seq):
    c = _tile(seq, HGRN_CHUNK)
    mall, lvl, nlev = _hgrn_tables(c)
    nc = seq // c
    col = lambda k: pl.BlockSpec((c, A_K), lambda b, i, k=k: (b * nc + i, k))
    const = lambda shape: pl.BlockSpec(shape, lambda b, i: (0,) * len(shape))
    return pl.pallas_call(
        functools.partial(_hgrn_kernel, c=c, nlev=nlev),
        grid=(batch, nc),
        in_specs=[col(0), col(1), col(2), col(3), const((1, A_K)), const((1, A_DV)),
                  const(mall.shape), const(lvl.shape)],
        out_specs=[pl.BlockSpec((c, A_V), lambda b, i: (b * nc + i, 0)),
                   pl.BlockSpec((1, A_HEADS, A_DK, A_DV), lambda b, i: (b, 0, 0, 0))],
        out_shape=[jax.ShapeDtypeStruct((batch * seq, A_V), BF16),
                   jax.ShapeDtypeStruct((batch, A_HEADS, A_DK, A_DV), F32)],
        scratch_shapes=[pltpu.VMEM((A_HEADS, A_DV, A_DK), F32)],
        compiler_params=_params("parallel", "arbitrary"),
    )(hh, hh, hh, hh, lb, a_norm, jnp.asarray(mall, BF16), jnp.asarray(lvl))


HGRN_STEP_ROWS = 8


def _hgrn_step_kernel(h_ref, s0_ref, lb_ref, an_ref, og_ref, s_ref):
    rows = HGRN_STEP_ROWS
    rid = lax.broadcasted_iota(jnp.int32, (rows, A_DK), 0)
    ones = jnp.ones((rows, A_DV), F32)
    an = an_ref[...]
    hp = lax.Precision.HIGHEST
    for h in range(A_HEADS):
        sl = lambda k: slice(k * A_K + h * A_DK, k * A_K + (h + 1) * A_DK)
        q = h_ref[:, sl(0)]
        qh = q * _sigmoid(q)
        f = h_ref[:, sl(1)]
        lb = lb_ref[:, h * A_DK:(h + 1) * A_DK]
        fg = lb + (1.0 - lb) * _sigmoid(f)
        kk = (1.0 - lb) * _sigmoid(-f)
        v = h_ref[:, sl(2)]
        o = jnp.zeros((rows, A_DV), F32)
        for r in range(rows):
            pick = rid == r
            fcol = lax.dot_general(jnp.where(pick, fg, 0.0), ones, _TN, precision=hp, preferred_element_type=F32)
            upd = lax.dot_general(jnp.where(pick, kk, 0.0), v, _TN, precision=hp, preferred_element_type=F32)
            s_new = fcol * s0_ref[0, r, h] + upd
            s_ref[r, h] = s_new
            o = o + jnp.dot(jnp.where(pick, qh, 0.0), s_new, precision=hp, preferred_element_type=F32)
        og_ref[:, h * A_DV:(h + 1) * A_DV] = (_rms(o) * an * _sigmoid(h_ref[:, sl(3)])).astype(og_ref.dtype)


def _hgrn_step(hh, state, layer, lb, a_norm):
    bs = hh.shape[0]
    rows = HGRN_STEP_ROWS
    return pl.pallas_call(
        _hgrn_step_kernel, grid=(bs // rows,),
        in_specs=[pl.BlockSpec((rows, 4 * A_K), lambda i: (i, 0)),
                  pl.BlockSpec((1, rows, A_HEADS, A_DK, A_DV), lambda i: (layer, i, 0, 0, 0)),
                  pl.BlockSpec((1, A_K), lambda i: (0, 0)), pl.BlockSpec((1, A_DV), lambda i: (0, 0))],
        out_specs=[pl.BlockSpec((rows, A_V), lambda i: (i, 0)),
                   pl.BlockSpec((rows, A_HEADS, A_DK, A_DV), lambda i: (i, 0, 0, 0))],
        out_shape=[jax.ShapeDtypeStruct((bs, A_V), BF16),
                   jax.ShapeDtypeStruct((bs, A_HEADS, A_DK, A_DV), F32)],
        compiler_params=_params("parallel"),
    )(hh, state, lb, a_norm)


POOL_HALO = 16


def _pool_project(m_groups, pw_ref, ps_ref, o_ref):
    for gi, m in enumerate(m_groups):
        sl = slice(gi * POOL_GW, (gi + 1) * POOL_GW)
        o_ref[:, sl] = (_dot(m.astype(BF16), pw_ref[gi]) * ps_ref[:, sl]).astype(o_ref.dtype)


def _pool_kernel(p_ref, pw_ref, ps_ref, o_ref, prev_scr, *, t):
    ti = pl.program_id(1)

    @pl.when(ti == 0)
    def _():
        prev_scr[...] = jnp.zeros_like(prev_scr)

    z = p_ref[...]
    zext = jnp.concatenate([prev_scr[...], z], axis=0)
    prev_scr[...] = z[t - POOL_HALO:, :]
    pos1 = (ti * t + 1 + lax.broadcasted_iota(jnp.int32, (t, 1), 0)).astype(F32)
    s = zext
    groups = []
    for gi, w in enumerate(POOL_WINDOWS):
        s = s[:, (POOL_GW if gi else 0):]
        s = s + pltpu.roll(s, w // 2, 0)
        sl = slice(gi * POOL_GW, (gi + 1) * POOL_GW)
        groups.append(s[POOL_HALO:, :POOL_GW] / jnp.minimum(pos1, float(w)) - z[:, sl])
    _pool_project(groups, pw_ref, ps_ref, o_ref)


def _pool_prompt(p, pool_w, pool_scale, batch, seq):
    t = _tile(seq, 512)
    nt = seq // t
    return pl.pallas_call(
        functools.partial(_pool_kernel, t=t), grid=(batch, nt),
        in_specs=[pl.BlockSpec((t, POOL_W), lambda b, i: (b * nt + i, 0)),
                  pl.BlockSpec(pool_w.shape, lambda b, i: (0, 0, 0)),
                  pl.BlockSpec((1, POOL_W), lambda b, i: (0, 0))],
        out_specs=pl.BlockSpec((t, POOL_W), lambda b, i: (b * nt + i, 0)),
        out_shape=jax.ShapeDtypeStruct((batch * seq, POOL_W), BF16),
        scratch_shapes=[pltpu.VMEM((POOL_HALO, POOL_W), F32)],
        compiler_params=_params("parallel", "arbitrary"),
    )(p, pool_w, pool_scale)


def _pool_step_kernel(p_ref, buf_ref, pw_ref, ps_ref, o_ref, nb_ref, *, past_len):
    z = p_ref[...]
    groups = []
    for gi, w in enumerate(POOL_WINDOWS):
        s = z[:, gi * POOL_GW:(gi + 1) * POOL_GW]
        for r in range(POOL_BUF - (w - 1), POOL_BUF):
            s = s + buf_ref[0, :, r * POOL_W + gi * POOL_GW:r * POOL_W + (gi + 1) * POOL_GW]
        groups.append(s / float(min(past_len + 1, w)) - z[:, gi * POOL_GW:(gi + 1) * POOL_GW])
    _pool_project(groups, pw_ref, ps_ref, o_ref)
    nb_ref[:, :(POOL_BUF - 1) * POOL_W] = buf_ref[0, :, POOL_W:]
    nb_ref[:, (POOL_BUF - 1) * POOL_W:] = z


def _pool_step(p, state_pool, layer, pool_w, pool_scale, past_len):
    bs = p.shape[0]
    flat = POOL_BUF * POOL_W
    pm, nb = pl.pallas_call(
        functools.partial(_pool_step_kernel, past_len=past_len), grid=(1,),
        in_specs=[pl.BlockSpec((bs, POOL_W), lambda i: (0, 0)),
                  pl.BlockSpec((1, bs, flat), lambda i: (layer, 0, 0)),
                  pl.BlockSpec(pool_w.shape, lambda i: (0, 0, 0)),
                  pl.BlockSpec((1, POOL_W), lambda i: (0, 0))],
        out_specs=[pl.BlockSpec((bs, POOL_W), lambda i: (0, 0)),
                   pl.BlockSpec((bs, flat), lambda i: (0, 0))],
        out_shape=[jax.ShapeDtypeStruct((bs, POOL_W), BF16),
                   jax.ShapeDtypeStruct((bs, flat), F32)],
        compiler_params=_params("arbitrary"),
    )(p, state_pool.reshape(state_pool.shape[0], bs, flat), pool_w, pool_scale)
    return pm, nb.reshape(bs, POOL_BUF, POOL_W)


MLA_IN = Q_LORA + KV_LORA + 2 * ROPE
Q_HEAD_COLS = NOPE + 2 * ROPE


def _rope(pair, cs):
    prod = pair * cs
    return prod[:, :ROPE] + prod[:, ROPE:]


def _mla_prep_kernel(h_ref, cs_ref, qn_ref, kn_ref, wuq_ref, wuk_ref, ckv_ref, kr_ref, kcat_ref, qcat_ref):
    cs = cs_ref[...]
    c_kv = _rms(h_ref[:, Q_LORA:Q_LORA + KV_LORA]) * kn_ref[...]
    k_rope = _rope(h_ref[:, Q_LORA + KV_LORA:], cs)
    ckv_ref[...] = c_kv
    kr_ref[...] = k_rope
    kcat_ref[:, :KV_LORA] = c_kv.astype(BF16)
    kcat_ref[:, KV_LORA:] = k_rope.astype(BF16)
    cq = (_rms(h_ref[:, :Q_LORA]) * qn_ref[...]).astype(BF16)
    qf = _dot(cq, wuq_ref[...])
    qscale = ATTN_SCALE * LOG2E
    for h in range(M_HEADS):
        base = h * Q_HEAD_COLS
        q_lat = _dot(qf[:, base:base + NOPE].astype(BF16), wuk_ref[h])
        q_rope = _rope(qf[:, base + NOPE:base + Q_HEAD_COLS], cs)
        qcat_ref[0, h, :, :KV_LORA] = (q_lat * qscale).astype(BF16)
        qcat_ref[0, h, :, KV_LORA:] = (q_rope * qscale).astype(BF16)


def _mla_prep(hm, cs, q_norm, kv_norm, wuq, wuk_t, batch, seq):
    t = _tile(seq, 512)
    nt = seq // t
    n = batch * seq
    row = lambda w: pl.BlockSpec((t, w), lambda b, i: (b * nt + i, 0))
    const = lambda shape: pl.BlockSpec(shape, lambda b, i: (0,) * len(shape))
    return pl.pallas_call(
        _mla_prep_kernel, grid=(batch, nt),
        in_specs=[row(MLA_IN), pl.BlockSpec((t, 2 * ROPE), lambda b, i: (i, 0)),
                  const((1, Q_LORA)), const((1, KV_LORA)), const(wuq.shape), const(wuk_t.shape)],
        out_specs=[row(KV_LORA), row(ROPE), row(QK_CAT),
                   pl.BlockSpec((1, M_HEADS, t, QK_CAT), lambda b, i: (b, 0, i, 0))],
        out_shape=[jax.ShapeDtypeStruct((n, KV_LORA), F32), jax.ShapeDtypeStruct((n, ROPE), F32),
                   jax.ShapeDtypeStruct((n, QK_CAT), BF16),
                   jax.ShapeDtypeStruct((batch, M_HEADS, seq, QK_CAT), BF16)],
        compiler_params=_params("parallel", "parallel"),
    )(hm, cs, q_norm, kv_norm, wuq, wuk_t)


LANES = 128
FLASH_SUB = 256


def _lane_tile(x, n):
    return jnp.concatenate([x] * n, axis=1)


def _flash_kernel(q_ref, k_ref, wuv_ref, o_ref, m_scr, l_scr, acc_scr, *, bq, sub):
    qi = pl.program_id(1)
    m_scr[...] = jnp.full_like(m_scr, -jnp.inf)
    l_scr[...] = jnp.zeros_like(l_scr)
    acc_scr[...] = jnp.zeros_like(acc_scr)

    def step(j, masked):
        kblk = k_ref[0, pl.ds(pl.multiple_of(j * bq, bq), bq), :]
        vblk = kblk[:, :KV_LORA]
        chains = [(h, u) for h in range(M_HEADS) for u in range(bq // sub)]

        def width(u):
            return (u + 1) * sub if masked else bq

        def scores(h, u):
            q = q_ref[0, h, u * sub:(u + 1) * sub, :]
            return lax.dot_general(q, kblk[:width(u)], _NT, preferred_element_type=F32)

        s_next = scores(*chains[0])
        for ci, (h, u) in enumerate(chains):
            s = s_next
            if ci + 1 < len(chains):
                s_next = scores(*chains[ci + 1])
            rows = pl.ds(h * bq + u * sub, sub)
            nk = width(u)
            if masked:
                tq = u * sub + lax.broadcasted_iota(jnp.int32, (sub, nk), 0)
                tk = lax.broadcasted_iota(jnp.int32, (sub, nk), 1)
                s = jnp.where(tk <= tq, s, -jnp.inf)
            m_prev = m_scr[rows, :]
            m_new = jnp.maximum(m_prev, jnp.max(s, axis=1, keepdims=True))
            alpha = jnp.exp2(m_prev - m_new)
            p = jnp.exp2(s - _lane_tile(m_new, nk // LANES))
            l_scr[rows, :] = alpha * l_scr[rows, :] + jnp.sum(p, axis=1, keepdims=True)
            acc_scr[rows, :] = (_lane_tile(alpha, KV_LORA // LANES) * acc_scr[rows, :]
                                + _dot(p.astype(BF16), vblk[:nk]))
            m_scr[rows, :] = m_new

    def body(j, carry):
        step(j, False)
        return carry

    lax.fori_loop(0, qi, body, 0)
    step(qi, True)
    for h in range(M_HEADS):
        rows = pl.ds(h * bq, bq)
        inv_l = _lane_tile(1.0 / l_scr[rows, :], KV_LORA // LANES)
        o_ref[:, h * M_VD:(h + 1) * M_VD] = _dot((acc_scr[rows, :] * inv_l).astype(BF16),
                                                 wuv_ref[h]).astype(o_ref.dtype)


def _flash_prompt(qcat, kcat, wuv, batch, seq):
    bq = _tile(seq, 512)
    nq = seq // bq
    rows = M_HEADS * bq
    return pl.pallas_call(
        functools.partial(_flash_kernel, bq=bq, sub=min(FLASH_SUB, bq)), grid=(batch, nq),
        in_specs=[pl.BlockSpec((1, M_HEADS, bq, QK_CAT), lambda b, i: (b, 0, i, 0)),
                  pl.BlockSpec((1, seq, QK_CAT), lambda b, i: (b, 0, 0)),
                  pl.BlockSpec(wuv.shape, lambda b, i: (0, 0, 0))],
        out_specs=pl.BlockSpec((bq, M_V), lambda b, i: (b * nq + i, 0)),
        out_shape=jax.ShapeDtypeStruct((batch * seq, M_V), BF16),
        scratch_shapes=[pltpu.VMEM((rows, LANES), F32), pltpu.VMEM((rows, LANES), F32),
                        pltpu.VMEM((rows, KV_LORA), F32)],
        compiler_params=_params("parallel", "arbitrary"),
    )(qcat, kcat.reshape(batch, seq, QK_CAT), wuv)


Q_ROWS = 8
PAGES_PER_STEP = 32
DECODE_SUB_KEYS = 1024


def _decode_kernel(pt_ref, q_ref, cnew_ref, rnew_ref, wuv_ref, ckv_hbm, kr_hbm, o_ref,
                   kbuf, rbuf, sem, m_scr, l_scr, acc_scr, *, layer, ppc, page):
    b = pl.program_id(0)
    c = pl.program_id(1)
    nb = pl.num_programs(0)
    nch = pl.num_programs(1)
    step = b * nch + c
    slot = step % 2

    def copies(bb, cc, sl, j):
        pg = pt_ref[bb, cc * ppc + j]
        keys = pl.ds(j * page, page)
        return (pltpu.make_async_copy(ckv_hbm.at[layer, pg], kbuf.at[sl, keys], sem.at[0, sl]),
                pltpu.make_async_copy(kr_hbm.at[layer, pg], rbuf.at[sl, :, keys], sem.at[1, sl]))

    def issue(bb, cc, sl):
        for j in range(ppc):
            for cp in copies(bb, cc, sl, j):
                cp.start()

    @pl.when(step == 0)
    def _():
        issue(b, c, slot)

    @pl.when(step + 1 < nb * nch)
    def _():
        nxt = step + 1
        issue(nxt // nch, nxt % nch, 1 - slot)

    q = q_ref[0]
    q_lat = q[:, :KV_LORA]
    q_rope = q[:, KV_LORA:]

    @pl.when(c == 0)
    def _():
        c_new = cnew_ref[0]
        m_scr[...] = (jnp.sum(q_lat.astype(F32) * c_new, axis=1, keepdims=True)
                      + jnp.sum(q_rope.astype(F32) * rnew_ref[0], axis=1, keepdims=True))
        l_scr[...] = jnp.ones_like(l_scr)
        acc_scr[...] = jnp.broadcast_to(c_new, acc_scr.shape)

    for j in range(ppc):
        for cp in copies(b, c, slot, j):
            cp.wait()

    sub = min(DECODE_SUB_KEYS, ppc * page)
    parts = [(m_scr[...], l_scr[...], acc_scr[...])]
    n_sub = ppc * page // sub
    kcs = [kbuf[slot, pl.ds(u * sub, sub), :].astype(BF16) for u in range(n_sub)]
    ss = [lax.dot_general(q_lat, kcs[u], _NT, preferred_element_type=F32)
          + _dot(q_rope, rbuf[slot, :, pl.ds(u * sub, sub)].astype(BF16)) for u in range(n_sub)]
    ms = [jnp.max(s, axis=1, keepdims=True) for s in ss]
    ps = [jnp.exp2(s - m) for s, m in zip(ss, ms)]
    for u in range(n_sub):
        parts.append((ms[u], jnp.sum(ps[u], axis=1, keepdims=True), _dot(ps[u].astype(BF16), kcs[u])))
    m_new = functools.reduce(jnp.maximum, [m for m, _, _ in parts])
    weights = [jnp.exp2(m - m_new) for m, _, _ in parts]
    m_scr[...] = m_new
    l_scr[...] = sum(w * l for w, (_, l, _) in zip(weights, parts))
    acc_scr[...] = sum(w * a for w, (_, _, a) in zip(weights, parts))

    @pl.when(c == nch - 1)
    def _():
        o_lat = (acc_scr[...] / l_scr[...]).astype(BF16)
        for h in range(M_HEADS):
            o_ref[0, :, h * M_VD:(h + 1) * M_VD] = _dot(o_lat, wuv_ref[h])[h:h + 1].astype(o_ref.dtype)


def _mla_decode(page_table, q8, c_new, r_new, wuv, cache_ckv, cache_krope_t, layer):
    bs, n_pages = page_table.shape
    page = cache_ckv.shape[2]
    ppc = PAGES_PER_STEP if n_pages % PAGES_PER_STEP == 0 else n_pages
    nch = n_pages // ppc
    grid_spec = pltpu.PrefetchScalarGridSpec(
        num_scalar_prefetch=1, grid=(bs, nch),
        in_specs=[pl.BlockSpec((1, Q_ROWS, QK_CAT), lambda b, c, pt: (b, 0, 0)),
                  pl.BlockSpec((1, 1, KV_LORA), lambda b, c, pt: (b, 0, 0)),
                  pl.BlockSpec((1, 1, ROPE), lambda b, c, pt: (b, 0, 0)),
                  pl.BlockSpec(wuv.shape, lambda b, c, pt: (0, 0, 0)),
                  pl.BlockSpec(memory_space=pl.ANY), pl.BlockSpec(memory_space=pl.ANY)],
        out_specs=pl.BlockSpec((1, 1, M_V), lambda b, c, pt: (b, 0, 0)),
        scratch_shapes=[pltpu.VMEM((2, ppc * page, KV_LORA), F32), pltpu.VMEM((2, ROPE, ppc * page), F32),
                        pltpu.SemaphoreType.DMA((2, 2)),
                        pltpu.VMEM((Q_ROWS, 1), F32), pltpu.VMEM((Q_ROWS, 1), F32),
                        pltpu.VMEM((Q_ROWS, KV_LORA), F32)])
    return pl.pallas_call(
        functools.partial(_decode_kernel, layer=layer, ppc=ppc, page=page),
        grid_spec=grid_spec, out_shape=jax.ShapeDtypeStruct((bs, 1, M_V), BF16),
        compiler_params=_params("arbitrary", "arbitrary"),
    )(page_table, q8, c_new, r_new, wuv, cache_ckv, cache_krope_t)


def _merge_kernel(og_ref, pm_ref, ov_ref, gt_ref, x_ref, wa_ref, wp_ref, wm_ref, wo_ref, g_ref, b_ref,
                  o_ref, ob_ref, *, alpha):
    d = D_MODEL
    merged = (gt_ref[:, :d].astype(F32) * _dot(og_ref[...], wa_ref[...])
              + gt_ref[:, d:2 * d].astype(F32) * _dot(pm_ref[...], wp_ref[...])
              + gt_ref[:, 2 * d:].astype(F32) * _dot(ov_ref[...], wm_ref[...]))
    y = alpha * x_ref[...] + _dot(merged.astype(BF16), wo_ref[...])
    out = _layernorm(y, g_ref[...], b_ref[...])
    o_ref[...] = out
    ob_ref[...] = out.astype(BF16)


def _merge(og, pm, ov, gates, x, wa, wp, wm, wo, g, b, alpha):
    n = x.shape[0]
    t = _tile(n, 512)
    row = lambda w: pl.BlockSpec((t, w), lambda i: (i, 0))
    const = lambda a: pl.BlockSpec(a.shape, lambda i: (0,) * a.ndim)
    return pl.pallas_call(
        functools.partial(_merge_kernel, alpha=alpha), grid=(n // t,),
        in_specs=[row(A_V), row(POOL_W), row(M_V), row(N_BRANCH * D_MODEL), row(D_MODEL),
                  const(wa), const(wp), const(wm), const(wo), const(g), const(b)],
        out_specs=[row(D_MODEL), row(D_MODEL)],
        out_shape=[jax.ShapeDtypeStruct((n, D_MODEL), F32), jax.ShapeDtypeStruct((n, D_MODEL), BF16)],
        compiler_params=_params("parallel"),
    )(og, pm, ov, gates, x, wa, wp, wm, wo, g, b)


FF_BLOCK = D_FF // 2


def _swiglu_block(xb, w1, w3, w2):
    a = _dot(xb, w1)
    return _dot((a * _sigmoid(a) * _dot(xb, w3)).astype(BF16), w2)


def _ffn_kernel(xb_ref, x_ref, w1_ref, w3_ref, w2_ref, g_ref, b_ref, o_ref, ob_ref, acc_scr, *, alpha):
    j = pl.program_id(1)

    @pl.when(j == 0)
    def _():
        acc_scr[...] = jnp.zeros_like(acc_scr)

    acc_scr[...] += _swiglu_block(xb_ref[...], w1_ref[...], w3_ref[...], w2_ref[...])

    @pl.when(j == pl.num_programs(1) - 1)
    def _():
        out = _layernorm(alpha * x_ref[...] + acc_scr[...], g_ref[...], b_ref[...])
        o_ref[...] = out
        ob_ref[...] = out.astype(BF16)


def _ffn(xb, x, w1, w3, w2, g, b, alpha):
    n = x.shape[0]
    t = _tile(n, 512)
    nf = D_FF // FF_BLOCK
    row = lambda: pl.BlockSpec((t, D_MODEL), lambda i, j: (i, 0))
    vec = lambda: pl.BlockSpec((1, D_MODEL), lambda i, j: (0, 0))
    return pl.pallas_call(
        functools.partial(_ffn_kernel, alpha=alpha), grid=(n // t, nf),
        in_specs=[row(), row(), pl.BlockSpec((D_MODEL, FF_BLOCK), lambda i, j: (0, j)),
                  pl.BlockSpec((D_MODEL, FF_BLOCK), lambda i, j: (0, j)),
                  pl.BlockSpec((FF_BLOCK, D_MODEL), lambda i, j: (j, 0)), vec(), vec()],
        out_specs=[row(), row()],
        out_shape=[jax.ShapeDtypeStruct((n, D_MODEL), F32), jax.ShapeDtypeStruct((n, D_MODEL), BF16)],
        scratch_shapes=[pltpu.VMEM((t, D_MODEL), F32)],
        compiler_params=_params("parallel", "arbitrary"),
    )(xb, x, w1, w3, w2, g, b)


TOP_K = 2


def _router_kernel(x_ref, rt_ref, idx_ref, w_ref):
    logits = lax.dot_general(rt_ref[...], x_ref[...], _NT, precision=lax.Precision.HIGHEST,
                             preferred_element_type=F32)
    eid = lax.broadcasted_iota(jnp.int32, logits.shape, 0)
    m1 = jnp.max(logits, axis=0, keepdims=True)
    i1 = jnp.min(jnp.where(logits == m1, eid, N_EXPERTS), axis=0, keepdims=True)
    rest = jnp.where(eid == i1, -jnp.inf, logits)
    m2 = jnp.max(rest, axis=0, keepdims=True)
    i2 = jnp.min(jnp.where(rest == m2, eid, N_EXPERTS), axis=0, keepdims=True)
    e2 = jnp.exp(m2 - m1)
    idx_ref[...] = jnp.concatenate([i1, i2], axis=0)
    w_ref[...] = jnp.concatenate([1.0 / (1.0 + e2), e2 / (1.0 + e2)], axis=0)


def _router(x, router_t):
    n = x.shape[0]
    t = _tile(n, 1024)
    return pl.pallas_call(
        _router_kernel, grid=(n // t,),
        in_specs=[pl.BlockSpec((t, D_MODEL), lambda i: (i, 0)),
                  pl.BlockSpec((N_EXPERTS, D_MODEL), lambda i: (0, 0))],
        out_specs=[pl.BlockSpec((TOP_K, t), lambda i: (0, i)), pl.BlockSpec((TOP_K, t), lambda i: (0, i))],
        out_shape=[jax.ShapeDtypeStruct((TOP_K, n), jnp.int32), jax.ShapeDtypeStruct((TOP_K, n), F32)],
        compiler_params=_params("parallel"),
    )(x, router_t)


MOE_ROWS = 256
COMBINE_ROWS = 256


def _row_copy(src_hbm, row, dst_buf, slot, i, sem):
    return pltpu.make_async_copy(src_hbm.at[pl.ds(row, 1)], dst_buf.at[slot, pl.ds(i, 1)], sem.at[slot])


def _gather_rows(step, n_steps, cur_ref, nxt_ref, src_hbm, buf, sem, n_rows):
    slot = step % 2

    @pl.when(step == 0)
    def _():
        for i in range(n_rows):
            _row_copy(src_hbm, cur_ref[0, 0, i], buf, 0, i, sem).start()

    for i in range(n_rows):
        _row_copy(src_hbm, nxt_ref[0, 0, i], buf, 1 - slot, i, sem).start()
    for i in range(n_rows):
        _row_copy(src_hbm, 0, buf, slot, i, sem).wait()

    @pl.when(step == n_steps - 1)
    def _():
        for i in range(n_rows):
            _row_copy(src_hbm, 0, buf, 1 - slot, i, sem).wait()

    return slot


def _experts_kernel(be_ref, cur_ref, nxt_ref, x_hbm, w1_ref, w3_ref, w2_ref, y_ref, xbuf, sem, *, rb):
    del be_ref
    slot = _gather_rows(pl.program_id(0), pl.num_programs(0), cur_ref, nxt_ref, x_hbm, xbuf, sem, rb)
    y_ref[...] = _swiglu_block(xbuf[slot].astype(BF16), w1_ref[0], w3_ref[0], w2_ref[0])


def _combine_kernel(cur_ref, nxt_ref, x_ref, w_ref, y_hbm, g_ref, b_ref, o_ref, ob_ref, ybuf, sem, *, tt, alpha):
    slot = _gather_rows(pl.program_id(0), pl.num_programs(0), cur_ref, nxt_ref, y_hbm, ybuf, sem, TOP_K * tt)
    w = w_ref[...]
    ffn = w[:, 0:1] * ybuf[slot, :tt, :] + w[:, 1:2] * ybuf[slot, tt:, :]
    out = _layernorm(alpha * x_ref[...] + ffn, g_ref[...], b_ref[...])
    o_ref[...] = out
    ob_ref[...] = out.astype(BF16)


def _moe_plan(idx_t, rb):
    n = idx_t.shape[1]
    n_assign = TOP_K * n
    p_total = n_assign + N_EXPERTS * rb
    ex = idx_t.reshape(n_assign)
    tok = jnp.tile(jnp.arange(n, dtype=jnp.int32), TOP_K)
    order = jnp.argsort(ex, stable=True)
    ex_sorted = ex[order]
    counts = jnp.sum((ex[None, :] == jnp.arange(N_EXPERTS, dtype=jnp.int32)[:, None]).astype(jnp.int32), axis=1)
    padded = ((counts + rb - 1) // rb) * rb
    group_end = jnp.cumsum(padded)
    group_start = group_end - padded
    first = jnp.cumsum(counts) - counts
    dest = group_start[ex_sorted] + jnp.arange(n_assign, dtype=jnp.int32) - first[ex_sorted]
    position = dest[jnp.argsort(order)]
    block_start = jnp.arange(p_total // rb, dtype=jnp.int32) * rb
    block_expert = jnp.minimum(jnp.searchsorted(group_end, block_start, side='right'),
                               N_EXPERTS - 1).astype(jnp.int32)
    row_expert = jnp.repeat(block_expert, rb)
    rank = jnp.arange(p_total, dtype=jnp.int32) - group_start[row_expert]
    source = jnp.clip(first[row_expert] + rank, 0, n_assign - 1)
    row_token = jnp.where((rank >= 0) & (rank < counts[row_expert]), tok[order][source], 0)
    return row_token.astype(jnp.int32), block_expert, position.reshape(TOP_K, n)


def _moe(x, idx_t, w_t, w1, w3, w2, g, b, alpha):
    n = x.shape[0]
    rb = min(MOE_ROWS, n)
    row_token, block_expert, position = _moe_plan(idx_t, rb)
    nb = row_token.shape[0] // rb
    rows3 = row_token.reshape(nb, 1, rb)
    smem = lambda width, nxt, last: pl.BlockSpec(
        (1, 1, width), lambda i, *_: (jnp.minimum(i + nxt, last), 0, 0), memory_space=pltpu.SMEM)
    weight = lambda shape: pl.BlockSpec((1,) + shape, lambda i, be: (be[i], 0, 0))
    y = pl.pallas_call(
        functools.partial(_experts_kernel, rb=rb),
        grid_spec=pltpu.PrefetchScalarGridSpec(
            num_scalar_prefetch=1, grid=(nb,),
            in_specs=[smem(rb, 0, nb - 1), smem(rb, 1, nb - 1), pl.BlockSpec(memory_space=pl.ANY),
                      weight((D_MODEL, D_FF)), weight((D_MODEL, D_FF)), weight((D_FF, D_MODEL))],
            out_specs=pl.BlockSpec((rb, D_MODEL), lambda i, be: (i, 0)),
            scratch_shapes=[pltpu.VMEM((2, rb, D_MODEL), F32), pltpu.SemaphoreType.DMA((2,))]),
        out_shape=jax.ShapeDtypeStruct((nb * rb, D_MODEL), F32),
        compiler_params=_params("arbitrary"),
    )(block_expert, rows3, rows3, x, w1, w3, w2)

    tt = min(COMBINE_ROWS, n)
    nt = n // tt
    pos3 = position.reshape(TOP_K, nt, tt).transpose(1, 0, 2).reshape(nt, 1, TOP_K * tt)
    row = lambda width: pl.BlockSpec((tt, width), lambda i: (i, 0))
    vec = lambda: pl.BlockSpec((1, D_MODEL), lambda i: (0, 0))
    return pl.pallas_call(
        functools.partial(_combine_kernel, tt=tt, alpha=alpha), grid=(nt,),
        in_specs=[smem(TOP_K * tt, 0, nt - 1), smem(TOP_K * tt, 1, nt - 1), row(D_MODEL), row(TOP_K),
                  pl.BlockSpec(memory_space=pl.ANY), vec(), vec()],
        out_specs=[row(D_MODEL), row(D_MODEL)],
        out_shape=[jax.ShapeDtypeStruct((n, D_MODEL), F32), jax.ShapeDtypeStruct((n, D_MODEL), BF16)],
        scratch_shapes=[pltpu.VMEM((2, TOP_K * tt, D_MODEL), F32), pltpu.SemaphoreType.DMA((2,))],
        compiler_params=_params("arbitrary"),
    )(pos3, pos3, x, w_t.T, y, g, b)


def _swap_halves_cols(w):
    half = w.shape[-1] // 2
    return jnp.concatenate([w[..., half:], w[..., :half]], axis=-1)


def _rope_table(pos):
    half = ROPE // 2
    inv = ROPE_BASE ** (-jnp.arange(half, dtype=F32) / half)
    ang = pos.astype(F32)[:, None] * inv[None, :]
    c, s = jnp.cos(ang), jnp.sin(ang)
    return jnp.concatenate([c, c, -s, s], axis=-1)


def _layer_weights(l, w_in, b_gate, lb, hgrn_norm, w_hgrn_out, pool_w, pool_scale, w_pool_out,
                   mla_q_norm, w_uq, mla_kv_norm, w_uk, w_uv, w_mla_out, w_o):
    offs = np.cumsum((0, A_K, A_K, A_V, A_V, POOL_W, Q_LORA, KV_LORA, ROPE))
    wi = w_in[l]
    w_kr = wi[:, offs[7]:offs[8]]
    uq = w_uq[l].reshape(Q_LORA, M_HEADS, NOPE + ROPE)
    uq = jnp.concatenate([uq, _swap_halves_cols(uq[..., NOPE:])], axis=-1)
    return dict(
        w_hgrn=wi[:, :offs[4]].astype(BF16),
        w_pool=wi[:, offs[4]:offs[5]].astype(BF16),
        w_mla=jnp.concatenate([wi[:, offs[5]:offs[8]], _swap_halves_cols(w_kr)], axis=1).astype(BF16),
        w_gate=wi[:, offs[8]:].astype(BF16),
        b_gate=b_gate[l].reshape(1, N_BRANCH * D_MODEL),
        lb=lb[l].reshape(1, A_K),
        a_norm=hgrn_norm[l].reshape(1, A_DV),
        w_a_out=w_hgrn_out[l].astype(BF16),
        pool_w=pool_w[l].astype(BF16),
        pool_scale=pool_scale[l].reshape(1, POOL_W),
        w_p_out=w_pool_out[l].astype(BF16),
        q_norm=mla_q_norm[l].reshape(1, Q_LORA),
        kv_norm=mla_kv_norm[l].reshape(1, KV_LORA),
        w_uq=uq.reshape(Q_LORA, M_HEADS * Q_HEAD_COLS).astype(BF16),
        w_uk_t=jnp.transpose(w_uk[l], (1, 2, 0)).astype(BF16),
        w_uv=jnp.transpose(w_uv[l], (1, 0, 2)).astype(BF16),
        w_m_out=w_mla_out[l].astype(BF16),
        w_o=w_o[l].astype(BF16),
    )


def _in_projections(xb, lw):
    hh = _matmul(xb, lw['w_hgrn'], F32, 1024)
    hp = _matmul(xb, lw['w_pool'], F32, POOL_W)
    hm = _matmul(xb, lw['w_mla'], F32, MLA_IN)
    gates = _matmul(xb, lw['w_gate'], BF16, 1024, bias=lw['b_gate'])
    return hh, hp, hm, gates


def kernel(x_prompt, x_sample, cache_ckv, cache_krope, state_hgrn, state_pool, page_table, w_in, b_gate, hgrn_lb_logits, hgrn_norm, w_hgrn_out, pool_w, pool_scale, w_pool_out, mla_q_norm, w_uq, mla_kv_norm, w_uk, w_uv, w_mla_out, w_o, ln1_g, ln1_b, ln2_g, ln2_b, ffn_w1, ffn_w3, ffn_w2, moe_router, moe_w1, moe_w3, moe_w2):
    bp, lp, _ = x_prompt.shape
    bs, ls, _ = x_sample.shape
    assert ls == 1, "the decode path handles one new token per sequence"
    depth = w_in.shape[0]
    past_len = page_table.shape[1] * cache_ckv.shape[2]
    alpha = (2 * depth) ** 0.25
    npr = bp * lp

    lb_cum = jnp.cumsum(jax.nn.softmax(hgrn_lb_logits.astype(F32), axis=0), axis=0)
    lb = lb_cum - lb_cum[0:1]
    cs_p = _rope_table(jnp.arange(lp, dtype=jnp.int32))
    cs_s = _rope_table(jnp.full((bs,), past_len, jnp.int32))
    vec = lambda a: a.reshape(1, D_MODEL)
    cache_krope_t = jnp.swapaxes(cache_krope, 2, 3)

    xp = x_prompt.reshape(npr, D_MODEL)
    xs = x_sample.reshape(bs, D_MODEL)
    xpb, xsb = xp.astype(BF16), xs.astype(BF16)
    outs = {k: [] for k in ('ckv_p', 'kr_p', 'ckv_s', 'kr_s', 'sh_p', 'sh_s', 'pb_p', 'pb_s')}
    for l in range(depth):
        lw = _layer_weights(l, w_in, b_gate, lb, hgrn_norm, w_hgrn_out, pool_w, pool_scale, w_pool_out,
                            mla_q_norm, w_uq, mla_kv_norm, w_uk, w_uv, w_mla_out, w_o)
        hh, hp, hm, gates = _in_projections(xpb, lw)
        og, s_p = _hgrn_prompt(hh, lw['lb'], lw['a_norm'], bp, lp)
        pm = _pool_prompt(hp, lw['pool_w'], lw['pool_scale'], bp, lp)
        ckv, kr, kcat, qcat = _mla_prep(hm, cs_p, lw['q_norm'], lw['kv_norm'], lw['w_uq'], lw['w_uk_t'], bp, lp)
        ov = _flash_prompt(qcat, kcat, lw['w_uv'], bp, lp)
        xp1, xp1b = _merge(og, pm, ov, gates, xp, lw['w_a_out'], lw['w_p_out'], lw['w_m_out'], lw['w_o'],
                           vec(ln1_g[l]), vec(ln1_b[l]), alpha)
        outs['ckv_p'].append(ckv.reshape(bp, lp, KV_LORA))
        outs['kr_p'].append(kr.reshape(bp, lp, ROPE))
        outs['sh_p'].append(s_p)
        outs['pb_p'].append(hp.reshape(bp, lp, POOL_W)[:, lp - POOL_BUF:])
        hh, hp, hm, gates = _in_projections(xsb, lw)
        og, s_s = _hgrn_step(hh, state_hgrn, l, lw['lb'], lw['a_norm'])
        pm, pb_s = _pool_step(hp, state_pool, l, lw['pool_w'], lw['pool_scale'], past_len)
        ckv, kr, _, qcat = _mla_prep(hm, cs_s, lw['q_norm'], lw['kv_norm'], lw['w_uq'], lw['w_uk_t'], 1, bs)
        q8 = jnp.pad(jnp.transpose(qcat[0], (1, 0, 2)), ((0, 0), (0, Q_ROWS - M_HEADS), (0, 0)))
        ckv = ckv.reshape(bs, 1, KV_LORA)
        kr = kr.reshape(bs, 1, ROPE)
        ov = _mla_decode(page_table, q8, ckv, kr, lw['w_uv'], cache_ckv, cache_krope_t, l).reshape(bs, M_V)
        xs1, xs1b = _merge(og, pm, ov, gates, xs, lw['w_a_out'], lw['w_p_out'], lw['w_m_out'], lw['w_o'],
                           vec(ln1_g[l]), vec(ln1_b[l]), alpha)
        outs['ckv_s'].append(ckv)
        outs['kr_s'].append(kr)
        outs['sh_s'].append(s_s)
        outs['pb_s'].append(pb_s)
        j = l // 2
        if l % 2 == 0:
            w1, w3, w2 = ffn_w1[j].astype(BF16), ffn_w3[j].astype(BF16), ffn_w2[j].astype(BF16)
            xp, xpb = _ffn(xp1b, xp1, w1, w3, w2, vec(ln2_g[l]), vec(ln2_b[l]), alpha)
            xs, xsb = _ffn(xs1b, xs1, w1, w3, w2, vec(ln2_g[l]), vec(ln2_b[l]), alpha)
        else:
            w1, w3, w2 = moe_w1[j].astype(BF16), moe_w3[j].astype(BF16), moe_w2[j].astype(BF16)
            router_t = moe_router[j].T
            xp, xpb = _moe(xp1, *_router(xp1, router_t), w1, w3, w2, vec(ln2_g[l]), vec(ln2_b[l]), alpha)
            xs, xsb = _moe(xs1, *_router(xs1, router_t), w1, w3, w2, vec(ln2_g[l]), vec(ln2_b[l]), alpha)

    st = lambda k: jnp.stack(outs[k])
    return (xp.reshape(bp, lp, D_MODEL), xs.reshape(bs, ls, D_MODEL), st('ckv_p'), st('kr_p'), st('ckv_s'),
            st('kr_s'), st('sh_p'), st('sh_s'), st('pb_p'), st('pb_s'))
```

```python
import functools
import math

import numpy as np
import jax
import jax.numpy as jnp
from jax import lax
from jax.experimental import pallas as pl
from jax.experimental.pallas import tpu as pltpu

D_MODEL = 1024
A_HEADS = 4
A_DK = 128
A_DV = 128
A_K = A_HEADS * A_DK
A_V = A_HEADS * A_DV
POOL_WINDOWS = (2, 4, 8, 16)
POOL_GW = 128
POOL_W = len(POOL_WINDOWS) * POOL_GW
POOL_BUF = max(POOL_WINDOWS) - 1
M_HEADS = 4
Q_LORA = 384
KV_LORA = 256
NOPE = 128
ROPE = 64
M_VD = 128
M_V = M_HEADS * M_VD
QK_CAT = KV_LORA + ROPE
ROPE_BASE = 10000.0
ATTN_SCALE = (NOPE + ROPE) ** -0.5
N_BRANCH = 3
D_FF = 2816
N_EXPERTS = 8
LN_EPS = 1e-5
RMS_EPS = 1e-6
LOG2E = 1.4426950408889634

BF16 = jnp.bfloat16
F32 = jnp.float32
VMEM_LIMIT = 56 * 2 ** 20

_NT = (((1,), (1,)), ((), ()))
_TN = (((0,), (0,)), ((), ()))


def _params(*sem):
    return pltpu.CompilerParams(dimension_semantics=sem, vmem_limit_bytes=VMEM_LIMIT)


def _tile(n, pref):
    return pref if n % pref == 0 else n


def _dot(a, b):
    return jnp.dot(a, b, preferred_element_type=F32)


def _sigmoid(x):
    return 1.0 / (1.0 + jnp.exp(-x))


def _layernorm(y, g, b):
    mu = jnp.mean(y, axis=-1, keepdims=True)
    d = y - mu
    var = jnp.mean(d * d, axis=-1, keepdims=True)
    return d * lax.rsqrt(var + LN_EPS) * g + b


def _rms(x):
    return x * lax.rsqrt(jnp.mean(x * x, axis=-1, keepdims=True) + RMS_EPS)


def _mm_kernel(x_ref, w_ref, o_ref):
    o_ref[...] = _dot(x_ref[...], w_ref[...]).astype(o_ref.dtype)


def _mm_gate_kernel(x_ref, w_ref, b_ref, o_ref):
    o_ref[...] = _sigmoid(_dot(x_ref[...], w_ref[...]) + b_ref[...]).astype(o_ref.dtype)


def _matmul(x, w, out_dtype, tn, bias=None):
    n, k = x.shape
    m = w.shape[1]
    tm = _tile(n, 1024)
    in_specs = [pl.BlockSpec((tm, k), lambda i, j: (i, 0)), pl.BlockSpec((k, tn), lambda i, j: (0, j))]
    args = [x, w]
    body = _mm_kernel
    if bias is not None:
        in_specs.append(pl.BlockSpec((1, tn), lambda i, j: (0, j)))
        args.append(bias)
        body = _mm_gate_kernel
    return pl.pallas_call(
        body, grid=(n // tm, m // tn), in_specs=in_specs,
        out_specs=pl.BlockSpec((tm, tn), lambda i, j: (i, j)),
        out_shape=jax.ShapeDtypeStruct((n, m), out_dtype),
        compiler_params=_params("parallel", "arbitrary"))(*args)


HGRN_CHUNK = 128
HGRN_SEQS = 1


def _hgrn_tables(c):
    nlev = int(math.log2(c))
    t = np.arange(c)
    u = t[None, :]
    mats = np.zeros((nlev + 2, c, c), np.float32)
    lvl = -np.ones((c, c), np.int32)
    for l in range(nlev):
        blk = c >> l
        s0 = (t // blk) * blk
        mid = s0 + blk // 2
        qrole = t >= mid
        row_q = (u > mid[:, None]) & (u <= t[:, None])
        row_k = (u > t[:, None]) & (u <= mid[:, None])
        mats[l] = np.where(qrole[:, None], row_q, row_k)
        pair = (s0[:, None] == s0[None, :]) & qrole[:, None] & (~qrole)[None, :]
        lvl[pair] = l
    lvl[t, t] = nlev
    mats[nlev] = u <= t[:, None]
    mats[nlev + 1] = u > t[:, None]
    return mats.reshape((nlev + 2) * c, c), lvl, nlev


def _hgrn_gates(f, lb):
    log_sig = jnp.minimum(f, 0.0) - jnp.log1p(jnp.exp(-jnp.abs(f)))
    b = jnp.log1p(-lb) + log_sig
    a = jnp.log(jnp.maximum(lb, 1e-37))
    both = jnp.maximum(a, b) + jnp.log1p(jnp.exp(-jnp.abs(a - b)))
    logf = jnp.where(lb > 0.0, both, b)
    kk = (1.0 - lb) * _sigmoid(-f)
    return logf, kk


def _hgrn_kernel(q_ref, f_ref, i_ref, g_ref, lb_ref, an_ref, mall_ref, lvl_ref, og_ref, s_ref, st_scr,
                 *, c, nlev):
    ci = pl.program_id(1)

    @pl.when(ci == 0)
    def _():
        st_scr[...] = jnp.zeros_like(st_scr)

    lvl = lvl_ref[...]
    an = an_ref[...]
    seqs = range(q_ref.shape[0])
    hsl = [slice(h * A_DK, (h + 1) * A_DK) for h in range(A_HEADS)]
    units = [(n, h) for n in seqs for h in range(A_HEADS)]
    qh, kk, e = [], [], []
    for n in seqs:
        q = q_ref[n].astype(F32)
        qh.append(q * _sigmoid(q))
        logf, k_n = _hgrn_gates(f_ref[n], lb_ref[...])
        kk.append(k_n)
        hi = logf.astype(BF16)
        mid = (logf - hi.astype(F32)).astype(BF16)
        d2 = _dot(mall_ref[...], jnp.concatenate([hi, mid], axis=1))
        e.append(jnp.exp(d2[:, :A_K] + d2[:, A_K:]))
    a = {u: jnp.zeros((c, c), F32) for u in units}
    for l in range(nlev + 1):
        for n, h in units:
            if l < nlev:
                el = e[n][l * c:(l + 1) * c, hsl[h]]
                x, y = qh[n][:, hsl[h]] * el, kk[n][:, hsl[h]] * el
            else:
                x, y = qh[n][:, hsl[h]], kk[n][:, hsl[h]]
            p = lax.dot_general(x.astype(BF16), y.astype(BF16), _NT, preferred_element_type=F32)
            a[n, h] = jnp.where(lvl == l, p, a[n, h])
    eb = [e[n][nlev * c:(nlev + 1) * c] for n in seqs]
    ke = [(kk[n] * e[n][(nlev + 1) * c:]).astype(BF16) for n in seqs]
    qe = [(qh[n] * eb[n]).astype(BF16) for n in seqs]
    v = [i_ref[n].astype(BF16) for n in seqs]
    gate = [_sigmoid(g_ref[n].astype(F32)) for n in seqs]
    for n, h in units:
        st = st_scr[n, h]
        o = _dot(a[n, h].astype(BF16), v[n][:, hsl[h]]) + lax.dot_general(
            qe[n][:, hsl[h]], st.astype(BF16), _NT, preferred_element_type=F32)
        st_new = st * eb[n][c - 1:c, hsl[h]] + lax.dot_general(v[n][:, hsl[h]], ke[n][:, hsl[h]], _TN,
                                                                preferred_element_type=F32)
        st_scr[n, h] = st_new
        og_ref[n, :, hsl[h]] = (_rms(o) * an * gate[n][:, hsl[h]]).astype(og_ref.dtype)

        @pl.when(ci == pl.num_programs(1) - 1)
        def _():
            s_ref[n, h] = st_new.T


def _hgrn_prompt(hh, lb, a_norm, batch, seq):
    c = _tile(seq, HGRN_CHUNK)
    mall, lvl, nlev = _hgrn_tables(c)
    nc = seq // c
    grp = HGRN_SEQS if batch % HGRN_SEQS == 0 else 1
    hh3 = hh.reshape(batch, seq, 4 * A_K)
    col = lambda k: pl.BlockSpec((grp, c, A_K), lambda b, i, k=k: (b, i, k))
    const = lambda shape: pl.BlockSpec(shape, lambda b, i: (0,) * len(shape))
    og, state = pl.pallas_call(
        functools.partial(_hgrn_kernel, c=c, nlev=nlev),
        grid=(batch // grp, nc),
        in_specs=[col(0), col(1), col(2), col(3), const((1, A_K)), const((1, A_DV)),
                  const(mall.shape), const(lvl.shape)],
        out_specs=[pl.BlockSpec((grp, c, A_V), lambda b, i: (b, i, 0)),
                   pl.BlockSpec((grp, A_HEADS, A_DK, A_DV), lambda b, i: (b, 0, 0, 0))],
        out_shape=[jax.ShapeDtypeStruct((batch, seq, A_V), BF16),
                   jax.ShapeDtypeStruct((batch, A_HEADS, A_DK, A_DV), F32)],
        scratch_shapes=[pltpu.VMEM((grp, A_HEADS, A_DV, A_DK), F32)],
        compiler_params=_params("parallel", "arbitrary"),
    )(hh3, hh3, hh3, hh3, lb, a_norm, jnp.asarray(mall, BF16), jnp.asarray(lvl))
    return og.reshape(batch * seq, A_V), state


HGRN_STEP_ROWS = 8


def _hgrn_step_kernel(h_ref, s0_ref, lb_ref, an_ref, og_ref, s_ref):
    rows = HGRN_STEP_ROWS
    rid = lax.broadcasted_iota(jnp.int32, (rows, A_DK), 0)
    ones = jnp.ones((rows, A_DV), F32)
    an = an_ref[...]
    hp = lax.Precision.HIGHEST
    for h in range(A_HEADS):
        sl = lambda k: slice(k * A_K + h * A_DK, k * A_K + (h + 1) * A_DK)
        q = h_ref[:, sl(0)]
        qh = q * _sigmoid(q)
        f = h_ref[:, sl(1)]
        lb = lb_ref[:, h * A_DK:(h + 1) * A_DK]
        fg = lb + (1.0 - lb) * _sigmoid(f)
        kk = (1.0 - lb) * _sigmoid(-f)
        v = h_ref[:, sl(2)]
        o = jnp.zeros((rows, A_DV), F32)
        for r in range(rows):
            pick = rid == r
            fcol = lax.dot_general(jnp.where(pick, fg, 0.0), ones, _TN, precision=hp, preferred_element_type=F32)
            upd = lax.dot_general(jnp.where(pick, kk, 0.0), v, _TN, precision=hp, preferred_element_type=F32)
            s_new = fcol * s0_ref[0, r, h] + upd
            s_ref[r, h] = s_new
            o = o + jnp.dot(jnp.where(pick, qh, 0.0), s_new, precision=hp, preferred_element_type=F32)
        og_ref[:, h * A_DV:(h + 1) * A_DV] = (_rms(o) * an * _sigmoid(h_ref[:, sl(3)])).astype(og_ref.dtype)


def _hgrn_step(hh, state, layer, lb, a_norm):
    bs = hh.shape[0]
    rows = HGRN_STEP_ROWS
    return pl.pallas_call(
        _hgrn_step_kernel, grid=(bs // rows,),
        in_specs=[pl.BlockSpec((rows, 4 * A_K), lambda i: (i, 0)),
                  pl.BlockSpec((1, rows, A_HEADS, A_DK, A_DV), lambda i: (layer, i, 0, 0, 0)),
                  pl.BlockSpec((1, A_K), lambda i: (0, 0)), pl.BlockSpec((1, A_DV), lambda i: (0, 0))],
        out_specs=[pl.BlockSpec((rows, A_V), lambda i: (i, 0)),
                   pl.BlockSpec((rows, A_HEADS, A_DK, A_DV), lambda i: (i, 0, 0, 0))],
        out_shape=[jax.ShapeDtypeStruct((bs, A_V), BF16),
                   jax.ShapeDtypeStruct((bs, A_HEADS, A_DK, A_DV), F32)],
        compiler_params=_params("parallel"),
    )(hh, state, lb, a_norm)


POOL_HALO = 16


def _pool_project(m_groups, pw_ref, ps_ref, o_ref):
    for gi, m in enumerate(m_groups):
        sl = slice(gi * POOL_GW, (gi + 1) * POOL_GW)
        o_ref[:, sl] = (_dot(m.astype(BF16), pw_ref[gi]) * ps_ref[:, sl]).astype(o_ref.dtype)


def _pool_kernel(p_ref, pw_ref, ps_ref, o_ref, prev_scr, *, t):
    ti = pl.program_id(1)

    @pl.when(ti == 0)
    def _():
        prev_scr[...] = jnp.zeros_like(prev_scr)

    z = p_ref[...]
    zext = jnp.concatenate([prev_scr[...], z], axis=0)
    prev_scr[...] = z[t - POOL_HALO:, :]
    pos1 = (ti * t + 1 + lax.broadcasted_iota(jnp.int32, (t, 1), 0)).astype(F32)
    s = zext
    groups = []
    for gi, w in enumerate(POOL_WINDOWS):
        s = s[:, (POOL_GW if gi else 0):]
        s = s + pltpu.roll(s, w // 2, 0)
        sl = slice(gi * POOL_GW, (gi + 1) * POOL_GW)
        groups.append(s[POOL_HALO:, :POOL_GW] / jnp.minimum(pos1, float(w)) - z[:, sl])
    _pool_project(groups, pw_ref, ps_ref, o_ref)


def _pool_prompt(p, pool_w, pool_scale, batch, seq):
    t = _tile(seq, 512)
    nt = seq // t
    return pl.pallas_call(
        functools.partial(_pool_kernel, t=t), grid=(batch, nt),
        in_specs=[pl.BlockSpec((t, POOL_W), lambda b, i: (b * nt + i, 0)),
                  pl.BlockSpec(pool_w.shape, lambda b, i: (0, 0, 0)),
                  pl.BlockSpec((1, POOL_W), lambda b, i: (0, 0))],
        out_specs=pl.BlockSpec((t, POOL_W), lambda b, i: (b * nt + i, 0)),
        out_shape=jax.ShapeDtypeStruct((batch * seq, POOL_W), BF16),
        scratch_shapes=[pltpu.VMEM((POOL_HALO, POOL_W), F32)],
        compiler_params=_params("parallel", "arbitrary"),
    )(p, pool_w, pool_scale)


def _pool_step_kernel(p_ref, buf_ref, pw_ref, ps_ref, o_ref, nb_ref, *, past_len):
    z = p_ref[...]
    groups = []
    for gi, w in enumerate(POOL_WINDOWS):
        s = z[:, gi * POOL_GW:(gi + 1) * POOL_GW]
        for r in range(POOL_BUF - (w - 1), POOL_BUF):
            s = s + buf_ref[0, :, r * POOL_W + gi * POOL_GW:r * POOL_W + (gi + 1) * POOL_GW]
        groups.append(s / float(min(past_len + 1, w)) - z[:, gi * POOL_GW:(gi + 1) * POOL_GW])
    _pool_project(groups, pw_ref, ps_ref, o_ref)
    nb_ref[:, :(POOL_BUF - 1) * POOL_W] = buf_ref[0, :, POOL_W:]
    nb_ref[:, (POOL_BUF - 1) * POOL_W:] = z


def _pool_step(p, state_pool, layer, pool_w, pool_scale, past_len):
    bs = p.shape[0]
    flat = POOL_BUF * POOL_W
    pm, nb = pl.pallas_call(
        functools.partial(_pool_step_kernel, past_len=past_len), grid=(1,),
        in_specs=[pl.BlockSpec((bs, POOL_W), lambda i: (0, 0)),
                  pl.BlockSpec((1, bs, flat), lambda i: (layer, 0, 0)),
                  pl.BlockSpec(pool_w.shape, lambda i: (0, 0, 0)),
                  pl.BlockSpec((1, POOL_W), lambda i: (0, 0))],
        out_specs=[pl.BlockSpec((bs, POOL_W), lambda i: (0, 0)),
                   pl.BlockSpec((bs, flat), lambda i: (0, 0))],
        out_shape=[jax.ShapeDtypeStruct((bs, POOL_W), BF16),
                   jax.ShapeDtypeStruct((bs, flat), F32)],
        compiler_params=_params("arbitrary"),
    )(p, state_pool.reshape(state_pool.shape[0], bs, flat), pool_w, pool_scale)
    return pm, nb.reshape(bs, POOL_BUF, POOL_W)


MLA_IN = Q_LORA + KV_LORA + 2 * ROPE
Q_HEAD_COLS = NOPE + 2 * ROPE


def _rope(pair, cs):
    prod = pair * cs
    return prod[:, :ROPE] + prod[:, ROPE:]


def _mla_prep_kernel(h_ref, cs_ref, qn_ref, kn_ref, wuq_ref, wuk_ref, ckv_ref, kr_ref, kcat_ref, qcat_ref):
    cs = cs_ref[...]
    c_kv = _rms(h_ref[:, Q_LORA:Q_LORA + KV_LORA]) * kn_ref[...]
    k_rope = _rope(h_ref[:, Q_LORA + KV_LORA:], cs)
    ckv_ref[...] = c_kv
    kr_ref[...] = k_rope
    kcat_ref[:, :KV_LORA] = c_kv.astype(BF16)
    kcat_ref[:, KV_LORA:] = k_rope.astype(BF16)
    cq = (_rms(h_ref[:, :Q_LORA]) * qn_ref[...]).astype(BF16)
    qf = _dot(cq, wuq_ref[...])
    qscale = ATTN_SCALE * LOG2E
    for h in range(M_HEADS):
        base = h * Q_HEAD_COLS
        q_lat = _dot(qf[:, base:base + NOPE].astype(BF16), wuk_ref[h])
        q_rope = _rope(qf[:, base + NOPE:base + Q_HEAD_COLS], cs)
        qcat_ref[0, h, :, :KV_LORA] = (q_lat * qscale).astype(BF16)
        qcat_ref[0, h, :, KV_LORA:] = (q_rope * qscale).astype(BF16)


def _mla_prep(hm, cs, q_norm, kv_norm, wuq, wuk_t, batch, seq):
    t = _tile(seq, 512)
    nt = seq // t
    n = batch * seq
    row = lambda w: pl.BlockSpec((t, w), lambda b, i: (b * nt + i, 0))
    const = lambda shape: pl.BlockSpec(shape, lambda b, i: (0,) * len(shape))
    return pl.pallas_call(
        _mla_prep_kernel, grid=(batch, nt),
        in_specs=[row(MLA_IN), pl.BlockSpec((t, 2 * ROPE), lambda b, i: (i, 0)),
                  const((1, Q_LORA)), const((1, KV_LORA)), const(wuq.shape), const(wuk_t.shape)],
        out_specs=[row(KV_LORA), row(ROPE), row(QK_CAT),
                   pl.BlockSpec((1, M_HEADS, t, QK_CAT), lambda b, i: (b, 0, i, 0))],
        out_shape=[jax.ShapeDtypeStruct((n, KV_LORA), F32), jax.ShapeDtypeStruct((n, ROPE), F32),
                   jax.ShapeDtypeStruct((n, QK_CAT), BF16),
                   jax.ShapeDtypeStruct((batch, M_HEADS, seq, QK_CAT), BF16)],
        compiler_params=_params("parallel", "parallel"),
    )(hm, cs, q_norm, kv_norm, wuq, wuk_t)


LANES = 128
FLASH_SUB = 256


def _lane_tile(x, n):
    return jnp.concatenate([x] * n, axis=1)


def _flash_kernel(q_ref, k_ref, wuv_ref, o_ref, m_scr, l_scr, acc_scr, *, bq, sub):
    qi = pl.program_id(1)
    m_scr[...] = jnp.full_like(m_scr, -jnp.inf)
    l_scr[...] = jnp.zeros_like(l_scr)
    acc_scr[...] = jnp.zeros_like(acc_scr)

    def step(j, masked):
        kblk = k_ref[0, pl.ds(pl.multiple_of(j * bq, bq), bq), :]
        vblk = kblk[:, :KV_LORA]
        chains = [(h, u) for h in range(M_HEADS) for u in range(bq // sub)]

        def width(u):
            return (u + 1) * sub if masked else bq

        def scores(h, u):
            q = q_ref[0, h, u * sub:(u + 1) * sub, :]
            return lax.dot_general(q, kblk[:width(u)], _NT, preferred_element_type=F32)

        s_next = scores(*chains[0])
        for ci, (h, u) in enumerate(chains):
            s = s_next
            if ci + 1 < len(chains):
                s_next = scores(*chains[ci + 1])
            rows = pl.ds(h * bq + u * sub, sub)
            nk = width(u)
            if masked:
                tq = u * sub + lax.broadcasted_iota(jnp.int32, (sub, nk), 0)
                tk = lax.broadcasted_iota(jnp.int32, (sub, nk), 1)
                s = jnp.where(tk <= tq, s, -jnp.inf)
            m_prev = m_scr[rows, :]
            m_new = jnp.maximum(m_prev, jnp.max(s, axis=1, keepdims=True))
            alpha = jnp.exp2(m_prev - m_new)
            p = jnp.exp2(s - _lane_tile(m_new, nk // LANES))
            l_scr[rows, :] = alpha * l_scr[rows, :] + jnp.sum(p, axis=1, keepdims=True)
            acc_scr[rows, :] = (_lane_tile(alpha, KV_LORA // LANES) * acc_scr[rows, :]
                                + _dot(p.astype(BF16), vblk[:nk]))
            m_scr[rows, :] = m_new

    def body(j, carry):
        step(j, False)
        return carry

    lax.fori_loop(0, qi, body, 0)
    step(qi, True)
    for h in range(M_HEADS):
        rows = pl.ds(h * bq, bq)
        inv_l = _lane_tile(1.0 / l_scr[rows, :], KV_LORA // LANES)
        o_ref[:, h * M_VD:(h + 1) * M_VD] = _dot((acc_scr[rows, :] * inv_l).astype(BF16),
                                                 wuv_ref[h]).astype(o_ref.dtype)


def _flash_prompt(qcat, kcat, wuv, batch, seq):
    bq = _tile(seq, 512)
    nq = seq // bq
    rows = M_HEADS * bq
    return pl.pallas_call(
        functools.partial(_flash_kernel, bq=bq, sub=min(FLASH_SUB, bq)), grid=(batch, nq),
        in_specs=[pl.BlockSpec((1, M_HEADS, bq, QK_CAT), lambda b, i: (b, 0, i, 0)),
                  pl.BlockSpec((1, seq, QK_CAT), lambda b, i: (b, 0, 0)),
                  pl.BlockSpec(wuv.shape, lambda b, i: (0, 0, 0))],
        out_specs=pl.BlockSpec((bq, M_V), lambda b, i: (b * nq + i, 0)),
        out_shape=jax.ShapeDtypeStruct((batch * seq, M_V), BF16),
        scratch_shapes=[pltpu.VMEM((rows, LANES), F32), pltpu.VMEM((rows, LANES), F32),
                        pltpu.VMEM((rows, KV_LORA), F32)],
        compiler_params=_params("parallel", "arbitrary"),
    )(qcat, kcat.reshape(batch, seq, QK_CAT), wuv)


Q_ROWS = 8
PAGES_PER_STEP = 64
DECODE_SUB_KEYS = 1024


def _decode_kernel(pt_ref, q_ref, cnew_ref, rnew_ref, wuv_ref, ckv_hbm, kr_hbm, o_ref,
                   kbuf, rbuf, sem, m_scr, l_scr, acc_scr, *, layer, ppc, page):
    b = pl.program_id(0)
    c = pl.program_id(1)
    nb = pl.num_programs(0)
    nch = pl.num_programs(1)
    step = b * nch + c
    slot = step % 2

    def copies(bb, cc, sl, j):
        pg = pt_ref[bb, cc * ppc + j]
        keys = pl.ds(j * page, page)
        return (pltpu.make_async_copy(ckv_hbm.at[layer, pg], kbuf.at[sl, keys], sem.at[0, sl]),
                pltpu.make_async_copy(kr_hbm.at[layer, pg], rbuf.at[sl, :, keys], sem.at[1, sl]))

    def issue(bb, cc, sl):
        for j in range(ppc):
            for cp in copies(bb, cc, sl, j):
                cp.start()

    @pl.when(step == 0)
    def _():
        issue(b, c, slot)

    @pl.when(step + 1 < nb * nch)
    def _():
        nxt = step + 1
        issue(nxt // nch, nxt % nch, 1 - slot)

    q = q_ref[0]
    q_lat = q[:, :KV_LORA]
    q_rope = q[:, KV_LORA:]

    @pl.when(c == 0)
    def _():
        c_new = cnew_ref[0]
        m_scr[...] = (jnp.sum(q_lat.astype(F32) * c_new, axis=1, keepdims=True)
                      + jnp.sum(q_rope.astype(F32) * rnew_ref[0], axis=1, keepdims=True))
        l_scr[...] = jnp.ones_like(l_scr)
        acc_scr[...] = jnp.broadcast_to(c_new, acc_scr.shape)

    for j in range(ppc):
        for cp in copies(b, c, slot, j):
            cp.wait()

    sub = min(DECODE_SUB_KEYS, ppc * page)
    parts = [(m_scr[...], l_scr[...], acc_scr[...])]
    n_sub = ppc * page // sub
    kcs = [kbuf[slot, pl.ds(u * sub, sub), :].astype(BF16) for u in range(n_sub)]
    ss = [lax.dot_general(q_lat, kcs[u], _NT, preferred_element_type=F32)
          + _dot(q_rope, rbuf[slot, :, pl.ds(u * sub, sub)].astype(BF16)) for u in range(n_sub)]
    ms = [jnp.max(s, axis=1, keepdims=True) for s in ss]
    ps = [jnp.exp2(s - m) for s, m in zip(ss, ms)]
    for u in range(n_sub):
        parts.append((ms[u], jnp.sum(ps[u], axis=1, keepdims=True), _dot(ps[u].astype(BF16), kcs[u])))
    m_new = functools.reduce(jnp.maximum, [m for m, _, _ in parts])
    weights = [jnp.exp2(m - m_new) for m, _, _ in parts]
    m_scr[...] = m_new
    l_scr[...] = sum(w * l for w, (_, l, _) in zip(weights, parts))
    acc_scr[...] = sum(w * a for w, (_, _, a) in zip(weights, parts))

    @pl.when(c == nch - 1)
    def _():
        o_lat = (acc_scr[...] / l_scr[...]).astype(BF16)
        for h in range(M_HEADS):
            o_ref[0, :, h * M_VD:(h + 1) * M_VD] = _dot(o_lat, wuv_ref[h])[h:h + 1].astype(o_ref.dtype)


def _mla_decode(page_table, q8, c_new, r_new, wuv, cache_ckv, cache_krope_t, layer):
    bs, n_pages = page_table.shape
    page = cache_ckv.shape[2]
    ppc = PAGES_PER_STEP if n_pages % PAGES_PER_STEP == 0 else n_pages
    nch = n_pages // ppc
    grid_spec = pltpu.PrefetchScalarGridSpec(
        num_scalar_prefetch=1, grid=(bs, nch),
        in_specs=[pl.BlockSpec((1, Q_ROWS, QK_CAT), lambda b, c, pt: (b, 0, 0)),
                  pl.BlockSpec((1, 1, KV_LORA), lambda b, c, pt: (b, 0, 0)),
                  pl.BlockSpec((1, 1, ROPE), lambda b, c, pt: (b, 0, 0)),
                  pl.BlockSpec(wuv.shape, lambda b, c, pt: (0, 0, 0)),
                  pl.BlockSpec(memory_space=pl.ANY), pl.BlockSpec(memory_space=pl.ANY)],
        out_specs=pl.BlockSpec((1, 1, M_V), lambda b, c, pt: (b, 0, 0)),
        scratch_shapes=[pltpu.VMEM((2, ppc * page, KV_LORA), F32), pltpu.VMEM((2, ROPE, ppc * page), F32),
                        pltpu.SemaphoreType.DMA((2, 2)),
                        pltpu.VMEM((Q_ROWS, 1), F32), pltpu.VMEM((Q_ROWS, 1), F32),
                        pltpu.VMEM((Q_ROWS, KV_LORA), F32)])
    return pl.pallas_call(
        functools.partial(_decode_kernel, layer=layer, ppc=ppc, page=page),
        grid_spec=grid_spec, out_shape=jax.ShapeDtypeStruct((bs, 1, M_V), BF16),
        compiler_params=_params("arbitrary", "arbitrary"),
    )(page_table, q8, c_new, r_new, wuv, cache_ckv, cache_krope_t)


def _merge_kernel(og_ref, pm_ref, ov_ref, gt_ref, x_ref, wa_ref, wp_ref, wm_ref, wo_ref, g_ref, b_ref,
                  o_ref, ob_ref, *, alpha):
    d = D_MODEL
    merged = (gt_ref[:, :d].astype(F32) * _dot(og_ref[...], wa_ref[...])
              + gt_ref[:, d:2 * d].astype(F32) * _dot(pm_ref[...], wp_ref[...])
              + gt_ref[:, 2 * d:].astype(F32) * _dot(ov_ref[...], wm_ref[...]))
    y = alpha * x_ref[...] + _dot(merged.astype(BF16), wo_ref[...])
    out = _layernorm(y, g_ref[...], b_ref[...])
    o_ref[...] = out
    ob_ref[...] = out.astype(BF16)


def _merge(og, pm, ov, gates, x, wa, wp, wm, wo, g, b, alpha):
    n = x.shape[0]
    t = _tile(n, 512)
    row = lambda w: pl.BlockSpec((t, w), lambda i: (i, 0))
    const = lambda a: pl.BlockSpec(a.shape, lambda i: (0,) * a.ndim)
    return pl.pallas_call(
        functools.partial(_merge_kernel, alpha=alpha), grid=(n // t,),
        in_specs=[row(A_V), row(POOL_W), row(M_V), row(N_BRANCH * D_MODEL), row(D_MODEL),
                  const(wa), const(wp), const(wm), const(wo), const(g), const(b)],
        out_specs=[row(D_MODEL), row(D_MODEL)],
        out_shape=[jax.ShapeDtypeStruct((n, D_MODEL), F32), jax.ShapeDtypeStruct((n, D_MODEL), BF16)],
        compiler_params=_params("parallel"),
    )(og, pm, ov, gates, x, wa, wp, wm, wo, g, b)


FF_BLOCK = D_FF // 2


def _swiglu_block(xb, w1, w3, w2):
    a = _dot(xb, w1)
    return _dot((a * _sigmoid(a) * _dot(xb, w3)).astype(BF16), w2)


def _ffn_kernel(xb_ref, x_ref, w1_ref, w3_ref, w2_ref, g_ref, b_ref, o_ref, ob_ref, acc_scr, *, alpha):
    j = pl.program_id(1)

    @pl.when(j == 0)
    def _():
        acc_scr[...] = jnp.zeros_like(acc_scr)

    acc_scr[...] += _swiglu_block(xb_ref[...], w1_ref[...], w3_ref[...], w2_ref[...])

    @pl.when(j == pl.num_programs(1) - 1)
    def _():
        out = _layernorm(alpha * x_ref[...] + acc_scr[...], g_ref[...], b_ref[...])
        o_ref[...] = out
        ob_ref[...] = out.astype(BF16)


def _ffn(xb, x, w1, w3, w2, g, b, alpha):
    n = x.shape[0]
    t = _tile(n, 512)
    nf = D_FF // FF_BLOCK
    row = lambda: pl.BlockSpec((t, D_MODEL), lambda i, j: (i, 0))
    vec = lambda: pl.BlockSpec((1, D_MODEL), lambda i, j: (0, 0))
    return pl.pallas_call(
        functools.partial(_ffn_kernel, alpha=alpha), grid=(n // t, nf),
        in_specs=[row(), row(), pl.BlockSpec((D_MODEL, FF_BLOCK), lambda i, j: (0, j)),
                  pl.BlockSpec((D_MODEL, FF_BLOCK), lambda i, j: (0, j)),
                  pl.BlockSpec((FF_BLOCK, D_MODEL), lambda i, j: (j, 0)), vec(), vec()],
        out_specs=[row(), row()],
        out_shape=[jax.ShapeDtypeStruct((n, D_MODEL), F32), jax.ShapeDtypeStruct((n, D_MODEL), BF16)],
        scratch_shapes=[pltpu.VMEM((t, D_MODEL), F32)],
        compiler_params=_params("parallel", "arbitrary"),
    )(xb, x, w1, w3, w2, g, b)


TOP_K = 2


def _router_kernel(x_ref, rt_ref, idx_ref, w_ref):
    logits = lax.dot_general(rt_ref[...], x_ref[...], _NT, precision=lax.Precision.HIGHEST,
                             preferred_element_type=F32)
    eid = lax.broadcasted_iota(jnp.int32, logits.shape, 0)
    m1 = jnp.max(logits, axis=0, keepdims=True)
    i1 = jnp.min(jnp.where(logits == m1, eid, N_EXPERTS), axis=0, keepdims=True)
    rest = jnp.where(eid == i1, -jnp.inf, logits)
    m2 = jnp.max(rest, axis=0, keepdims=True)
    i2 = jnp.min(jnp.where(rest == m2, eid, N_EXPERTS), axis=0, keepdims=True)
    e2 = jnp.exp(m2 - m1)
    idx_ref[...] = jnp.concatenate([i1, i2], axis=0)
    w_ref[...] = jnp.concatenate([1.0 / (1.0 + e2), e2 / (1.0 + e2)], axis=0)


def _router(x, router_t):
    n = x.shape[0]
    t = _tile(n, 1024)
    return pl.pallas_call(
        _router_kernel, grid=(n // t,),
        in_specs=[pl.BlockSpec((t, D_MODEL), lambda i: (i, 0)),
                  pl.BlockSpec((N_EXPERTS, D_MODEL), lambda i: (0, 0))],
        out_specs=[pl.BlockSpec((TOP_K, t), lambda i: (0, i)), pl.BlockSpec((TOP_K, t), lambda i: (0, i))],
        out_shape=[jax.ShapeDtypeStruct((TOP_K, n), jnp.int32), jax.ShapeDtypeStruct((TOP_K, n), F32)],
        compiler_params=_params("parallel"),
    )(x, router_t)


MOE_ROWS = 256
COMBINE_ROWS = 256


def _row_copy(src_hbm, row, dst_buf, slot, i, sem):
    return pltpu.make_async_copy(src_hbm.at[pl.ds(row, 1)], dst_buf.at[slot, pl.ds(i, 1)], sem.at[slot])


def _gather_rows(step, n_steps, cur_ref, nxt_ref, src_hbm, buf, sem, n_rows):
    slot = step % 2

    @pl.when(step == 0)
    def _():
        for i in range(n_rows):
            _row_copy(src_hbm, cur_ref[0, 0, i], buf, 0, i, sem).start()

    for i in range(n_rows):
        _row_copy(src_hbm, nxt_ref[0, 0, i], buf, 1 - slot, i, sem).start()
    for i in range(n_rows):
        _row_copy(src_hbm, 0, buf, slot, i, sem).wait()

    @pl.when(step == n_steps - 1)
    def _():
        for i in range(n_rows):
            _row_copy(src_hbm, 0, buf, 1 - slot, i, sem).wait()

    return slot


def _experts_kernel(be_ref, cur_ref, nxt_ref, x_hbm, w1_ref, w3_ref, w2_ref, y_ref, xbuf, sem, *, rb):
    del be_ref
    slot = _gather_rows(pl.program_id(0), pl.num_programs(0), cur_ref, nxt_ref, x_hbm, xbuf, sem, rb)
    y_ref[...] = _swiglu_block(xbuf[slot].astype(BF16), w1_ref[0], w3_ref[0], w2_ref[0])


def _combine_kernel(cur_ref, nxt_ref, x_ref, w_ref, y_hbm, g_ref, b_ref, o_ref, ob_ref, ybuf, sem, *, tt, alpha):
    slot = _gather_rows(pl.program_id(0), pl.num_programs(0), cur_ref, nxt_ref, y_hbm, ybuf, sem, TOP_K * tt)
    w = w_ref[...]
    ffn = w[:, 0:1] * ybuf[slot, :tt, :] + w[:, 1:2] * ybuf[slot, tt:, :]
    out = _layernorm(alpha * x_ref[...] + ffn, g_ref[...], b_ref[...])
    o_ref[...] = out
    ob_ref[...] = out.astype(BF16)


def _moe_plan(idx_t, rb):
    n = idx_t.shape[1]
    n_assign = TOP_K * n
    p_total = n_assign + N_EXPERTS * rb
    ex = idx_t.reshape(n_assign)
    tok = jnp.tile(jnp.arange(n, dtype=jnp.int32), TOP_K)
    order = jnp.argsort(ex, stable=True)
    ex_sorted = ex[order]
    counts = jnp.sum((ex[None, :] == jnp.arange(N_EXPERTS, dtype=jnp.int32)[:, None]).astype(jnp.int32), axis=1)
    padded = ((counts + rb - 1) // rb) * rb
    group_end = jnp.cumsum(padded)
    group_start = group_end - padded
    first = jnp.cumsum(counts) - counts
    dest = group_start[ex_sorted] + jnp.arange(n_assign, dtype=jnp.int32) - first[ex_sorted]
    position = dest[jnp.argsort(order)]
    block_start = jnp.arange(p_total // rb, dtype=jnp.int32) * rb
    block_expert = jnp.minimum(jnp.searchsorted(group_end, block_start, side='right'),
                               N_EXPERTS - 1).astype(jnp.int32)
    row_expert = jnp.repeat(block_expert, rb)
    rank = jnp.arange(p_total, dtype=jnp.int32) - group_start[row_expert]
    source = jnp.clip(first[row_expert] + rank, 0, n_assign - 1)
    row_token = jnp.where((rank >= 0) & (rank < counts[row_expert]), tok[order][source], 0)
    return row_token.astype(jnp.int32), block_expert, position.reshape(TOP_K, n)


def _moe(x, idx_t, w_t, w1, w3, w2, g, b, alpha):
    n = x.shape[0]
    rb = min(MOE_ROWS, n)
    row_token, block_expert, position = _moe_plan(idx_t, rb)
    nb = row_token.shape[0] // rb
    rows3 = row_token.reshape(nb, 1, rb)
    smem = lambda width, nxt, last: pl.BlockSpec(
        (1, 1, width), lambda i, *_: (jnp.minimum(i + nxt, last), 0, 0), memory_space=pltpu.SMEM)
    weight = lambda shape: pl.BlockSpec((1,) + shape, lambda i, be: (be[i], 0, 0))
    y = pl.pallas_call(
        functools.partial(_experts_kernel, rb=rb),
        grid_spec=pltpu.PrefetchScalarGridSpec(
            num_scalar_prefetch=1, grid=(nb,),
            in_specs=[smem(rb, 0, nb - 1), smem(rb, 1, nb - 1), pl.BlockSpec(memory_space=pl.ANY),
                      weight((D_MODEL, D_FF)), weight((D_MODEL, D_FF)), weight((D_FF, D_MODEL))],
            out_specs=pl.BlockSpec((rb, D_MODEL), lambda i, be: (i, 0)),
            scratch_shapes=[pltpu.VMEM((2, rb, D_MODEL), F32), pltpu.SemaphoreType.DMA((2,))]),
        out_shape=jax.ShapeDtypeStruct((nb * rb, D_MODEL), F32),
        compiler_params=_params("arbitrary"),
    )(block_expert, rows3, rows3, x, w1, w3, w2)

    tt = min(COMBINE_ROWS, n)
    nt = n // tt
    pos3 = position.reshape(TOP_K, nt, tt).transpose(1, 0, 2).reshape(nt, 1, TOP_K * tt)
    row = lambda width: pl.BlockSpec((tt, width), lambda i: (i, 0))
    vec = lambda: pl.BlockSpec((1, D_MODEL), lambda i: (0, 0))
    return pl.pallas_call(
        functools.partial(_combine_kernel, tt=tt, alpha=alpha), grid=(nt,),
        in_specs=[smem(TOP_K * tt, 0, nt - 1), smem(TOP_K * tt, 1, nt - 1), row(D_MODEL), row(TOP_K),
                  pl.BlockSpec(memory_space=pl.ANY), vec(), vec()],
        out_specs=[row(D_MODEL), row(D_MODEL)],
        out_shape=[jax.ShapeDtypeStruct((n, D_MODEL), F32), jax.ShapeDtypeStruct((n, D_MODEL), BF16)],
        scratch_shapes=[pltpu.VMEM((2, TOP_K * tt, D_MODEL), F32), pltpu.SemaphoreType.DMA((2,))],
        compiler_params=_params("arbitrary"),
    )(pos3, pos3, x, w_t.T, y, g, b)


def _swap_halves_cols(w):
    half = w.shape[-1] // 2
    return jnp.concatenate([w[..., half:], w[..., :half]], axis=-1)


def _rope_table(pos):
    half = ROPE // 2
    inv = ROPE_BASE ** (-jnp.arange(half, dtype=F32) / half)
    ang = pos.astype(F32)[:, None] * inv[None, :]
    c, s = jnp.cos(ang), jnp.sin(ang)
    return jnp.concatenate([c, c, -s, s], axis=-1)


def _layer_weights(l, w_in, b_gate, lb, hgrn_norm, w_hgrn_out, pool_w, pool_scale, w_pool_out,
                   mla_q_norm, w_uq, mla_kv_norm, w_uk, w_uv, w_mla_out, w_o):
    offs = np.cumsum((0, A_K, A_K, A_V, A_V, POOL_W, Q_LORA, KV_LORA, ROPE))
    wi = w_in[l]
    w_kr = wi[:, offs[7]:offs[8]]
    uq = w_uq[l].reshape(Q_LORA, M_HEADS, NOPE + ROPE)
    uq = jnp.concatenate([uq, _swap_halves_cols(uq[..., NOPE:])], axis=-1)
    return dict(
        w_hgrn=wi[:, :offs[4]].astype(BF16),
        w_pool=wi[:, offs[4]:offs[5]].astype(BF16),
        w_mla=jnp.concatenate([wi[:, offs[5]:offs[8]], _swap_halves_cols(w_kr)], axis=1).astype(BF16),
        w_gate=wi[:, offs[8]:].astype(BF16),
        b_gate=b_gate[l].reshape(1, N_BRANCH * D_MODEL),
        lb=lb[l].reshape(1, A_K),
        a_norm=hgrn_norm[l].reshape(1, A_DV),
        w_a_out=w_hgrn_out[l].astype(BF16),
        pool_w=pool_w[l].astype(BF16),
        pool_scale=pool_scale[l].reshape(1, POOL_W),
        w_p_out=w_pool_out[l].astype(BF16),
        q_norm=mla_q_norm[l].reshape(1, Q_LORA),
        kv_norm=mla_kv_norm[l].reshape(1, KV_LORA),
        w_uq=uq.reshape(Q_LORA, M_HEADS * Q_HEAD_COLS).astype(BF16),
        w_uk_t=jnp.transpose(w_uk[l], (1, 2, 0)).astype(BF16),
        w_uv=jnp.transpose(w_uv[l], (1, 0, 2)).astype(BF16),
        w_m_out=w_mla_out[l].astype(BF16),
        w_o=w_o[l].astype(BF16),
    )


def _in_projections(xb, lw):
    hh = _matmul(xb, lw['w_hgrn'], F32, 1024)
    hp = _matmul(xb, lw['w_pool'], F32, POOL_W)
    hm = _matmul(xb, lw['w_mla'], F32, MLA_IN)
    gates = _matmul(xb, lw['w_gate'], BF16, 1024, bias=lw['b_gate'])
    return hh, hp, hm, gates


def kernel(x_prompt, x_sample, cache_ckv, cache_krope, state_hgrn, state_pool, page_table, w_in, b_gate, hgrn_lb_logits, hgrn_norm, w_hgrn_out, pool_w, pool_scale, w_pool_out, mla_q_norm, w_uq, mla_kv_norm, w_uk, w_uv, w_mla_out, w_o, ln1_g, ln1_b, ln2_g, ln2_b, ffn_w1, ffn_w3, ffn_w2, moe_router, moe_w1, moe_w3, moe_w2):
    bp, lp, _ = x_prompt.shape
    bs, ls, _ = x_sample.shape
    assert ls == 1, "the decode path handles one new token per sequence"
    depth = w_in.shape[0]
    past_len = page_table.shape[1] * cache_ckv.shape[2]
    alpha = (2 * depth) ** 0.25
    npr = bp * lp

    lb_cum = jnp.cumsum(jax.nn.softmax(hgrn_lb_logits.astype(F32), axis=0), axis=0)
    lb = lb_cum - lb_cum[0:1]
    cs_p = _rope_table(jnp.arange(lp, dtype=jnp.int32))
    cs_s = _rope_table(jnp.full((bs,), past_len, jnp.int32))
    vec = lambda a: a.reshape(1, D_MODEL)
    cache_krope_t = jnp.swapaxes(cache_krope, 2, 3)

    xp = x_prompt.reshape(npr, D_MODEL)
    xs = x_sample.reshape(bs, D_MODEL)
    xpb, xsb = xp.astype(BF16), xs.astype(BF16)
    outs = {k: [] for k in ('ckv_p', 'kr_p', 'ckv_s', 'kr_s', 'sh_p', 'sh_s', 'pb_p', 'pb_s')}
    for l in range(depth):
        lw = _layer_weights(l, w_in, b_gate, lb, hgrn_norm, w_hgrn_out, pool_w, pool_scale, w_pool_out,
                            mla_q_norm, w_uq, mla_kv_norm, w_uk, w_uv, w_mla_out, w_o)
        hh, hp, hm, gates = _in_projections(xpb, lw)
        og, s_p = _hgrn_prompt(hh, lw['lb'], lw['a_norm'], bp, lp)
        pm = _pool_prompt(hp, lw['pool_w'], lw['pool_scale'], bp, lp)
        ckv, kr, kcat, qcat = _mla_prep(hm, cs_p, lw['q_norm'], lw['kv_norm'], lw['w_uq'], lw['w_uk_t'], bp, lp)
        ov = _flash_prompt(qcat, kcat, lw['w_uv'], bp, lp)
        xp1, xp1b = _merge(og, pm, ov, gates, xp, lw['w_a_out'], lw['w_p_out'], lw['w_m_out'], lw['w_o'],
                           vec(ln1_g[l]), vec(ln1_b[l]), alpha)
        outs['ckv_p'].append(ckv.reshape(bp, lp, KV_LORA))
        outs['kr_p'].append(kr.reshape(bp, lp, ROPE))
        outs['sh_p'].append(s_p)
        outs['pb_p'].append(hp.reshape(bp, lp, POOL_W)[:, lp - POOL_BUF:])
        hh, hp, hm, gates = _in_projections(xsb, lw)
        og, s_s = _hgrn_step(hh, state_hgrn, l, lw['lb'], lw['a_norm'])
        pm, pb_s = _pool_step(hp, state_pool, l, lw['pool_w'], lw['pool_scale'], past_len)
        ckv, kr, _, qcat = _mla_prep(hm, cs_s, lw['q_norm'], lw['kv_norm'], lw['w_uq'], lw['w_uk_t'], 1, bs)
        q8 = jnp.pad(jnp.transpose(qcat[0], (1, 0, 2)), ((0, 0), (0, Q_ROWS - M_HEADS), (0, 0)))
        ckv = ckv.reshape(bs, 1, KV_LORA)
        kr = kr.reshape(bs, 1, ROPE)
        ov = _mla_decode(page_table, q8, ckv, kr, lw['w_uv'], cache_ckv, cache_krope_t, l).reshape(bs, M_V)
        xs1, xs1b = _merge(og, pm, ov, gates, xs, lw['w_a_out'], lw['w_p_out'], lw['w_m_out'], lw['w_o'],
                           vec(ln1_g[l]), vec(ln1_b[l]), alpha)
        outs['ckv_s'].append(ckv)
        outs['kr_s'].append(kr)
        outs['sh_s'].append(s_s)
        outs['pb_s'].append(pb_s)
        j = l // 2
        if l % 2 == 0:
            w1, w3, w2 = ffn_w1[j].astype(BF16), ffn_w3[j].astype(BF16), ffn_w2[j].astype(BF16)
            xp, xpb = _ffn(xp1b, xp1, w1, w3, w2, vec(ln2_g[l]), vec(ln2_b[l]), alpha)
            xs, xsb = _ffn(xs1b, xs1, w1, w3, w2, vec(ln2_g[l]), vec(ln2_b[l]), alpha)
        else:
            w1, w3, w2 = moe_w1[j].astype(BF16), moe_w3[j].astype(BF16), moe_w2[j].astype(BF16)
            router_t = moe_router[j].T
            xp, xpb = _moe(xp1, *_router(xp1, router_t), w1, w3, w2, vec(ln2_g[l]), vec(ln2_b[l]), alpha)
            xs, xsb = _moe(xs1, *_router(xs1, router_t), w1, w3, w2, vec(ln2_g[l]), vec(ln2_b[l]), alpha)

    st = lambda k: jnp.stack(outs[k])
    return (xp.reshape(bp, lp, D_MODEL), xs.reshape(bs, ls, D_MODEL), st('ckv_p'), st('kr_p'), st('ckv_s'),
            st('kr_s'), st('sh_p'), st('sh_s'), st('pb_p'), st('pb_s'))
```

```python
import functools
import math

import numpy as np
import jax
import jax.numpy as jnp
from jax import lax
from jax.experimental import pallas as pl
from jax.experimental.pallas import tpu as pltpu

D_MODEL = 1024
A_HEADS = 4
A_DK = 128
A_DV = 128
A_K = A_HEADS * A_DK
A_V = A_HEADS * A_DV
POOL_WINDOWS = (2, 4, 8, 16)
POOL_GW = 128
POOL_W = len(POOL_WINDOWS) * POOL_GW
POOL_BUF = max(POOL_WINDOWS) - 1
M_HEADS = 4
Q_LORA = 384
KV_LORA = 256
NOPE = 128
ROPE = 64
M_VD = 128
M_V = M_HEADS * M_VD
QK_CAT = KV_LORA + ROPE
ROPE_BASE = 10000.0
ATTN_SCALE = (NOPE + ROPE) ** -0.5
N_BRANCH = 3
D_FF = 2816
N_EXPERTS = 8
LN_EPS = 1e-5
RMS_EPS = 1e-6
LOG2E = 1.4426950408889634

BF16 = jnp.bfloat16
F32 = jnp.float32
VMEM_LIMIT = 56 * 2 ** 20

_NT = (((1,), (1,)), ((), ()))
_TN = (((0,), (0,)), ((), ()))


def _params(*sem):
    return pltpu.CompilerParams(dimension_semantics=sem, vmem_limit_bytes=VMEM_LIMIT)


def _tile(n, pref):
    return pref if n % pref == 0 else n


def _dot(a, b):
    return jnp.dot(a, b, preferred_element_type=F32)


def _sigmoid(x):
    return 1.0 / (1.0 + jnp.exp(-x))


def _layernorm(y, g, b):
    mu = jnp.mean(y, axis=-1, keepdims=True)
    d = y - mu
    var = jnp.mean(d * d, axis=-1, keepdims=True)
    return d * lax.rsqrt(var + LN_EPS) * g + b


def _rms(x):
    return x * lax.rsqrt(jnp.mean(x * x, axis=-1, keepdims=True) + RMS_EPS)


def _mm_kernel(x_ref, w_ref, o_ref):
    o_ref[...] = _dot(x_ref[...], w_ref[...]).astype(o_ref.dtype)


def _mm_gate_kernel(x_ref, w_ref, b_ref, o_ref):
    o_ref[...] = _sigmoid(_dot(x_ref[...], w_ref[...]) + b_ref[...]).astype(o_ref.dtype)


def _matmul(x, w, out_dtype, tn, bias=None):
    n, k = x.shape
    m = w.shape[1]
    tm = _tile(n, 1024)
    in_specs = [pl.BlockSpec((tm, k), lambda i, j: (i, 0)), pl.BlockSpec((k, tn), lambda i, j: (0, j))]
    args = [x, w]
    body = _mm_kernel
    if bias is not None:
        in_specs.append(pl.BlockSpec((1, tn), lambda i, j: (0, j)))
        args.append(bias)
        body = _mm_gate_kernel
    return pl.pallas_call(
        body, grid=(n // tm, m // tn), in_specs=in_specs,
        out_specs=pl.BlockSpec((tm, tn), lambda i, j: (i, j)),
        out_shape=jax.ShapeDtypeStruct((n, m), out_dtype),
        compiler_params=_params("parallel", "arbitrary"))(*args)


HGRN_CHUNK = 128
HGRN_SEQS = 1


def _hgrn_tables(c):
    nlev = int(math.log2(c))
    t = np.arange(c)
    u = t[None, :]
    mats = np.zeros((nlev + 2, c, c), np.float32)
    lvl = -np.ones((c, c), np.int32)
    for l in range(nlev):
        blk = c >> l
        s0 = (t // blk) * blk
        mid = s0 + blk // 2
        qrole = t >= mid
        row_q = (u > mid[:, None]) & (u <= t[:, None])
        row_k = (u > t[:, None]) & (u <= mid[:, None])
        mats[l] = np.where(qrole[:, None], row_q, row_k)
        pair = (s0[:, None] == s0[None, :]) & qrole[:, None] & (~qrole)[None, :]
        lvl[pair] = l
    lvl[t, t] = nlev
    mats[nlev] = u <= t[:, None]
    mats[nlev + 1] = u > t[:, None]
    return mats.reshape((nlev + 2) * c, c), lvl, nlev


def _hgrn_gates(f, lb):
    log_sig = jnp.minimum(f, 0.0) - jnp.log1p(jnp.exp(-jnp.abs(f)))
    b = jnp.log1p(-lb) + log_sig
    a = jnp.log(jnp.maximum(lb, 1e-37))
    both = jnp.maximum(a, b) + jnp.log1p(jnp.exp(-jnp.abs(a - b)))
    logf = jnp.where(lb > 0.0, both, b)
    kk = (1.0 - lb) * _sigmoid(-f)
    return logf, kk


def _hgrn_kernel(q_ref, f_ref, i_ref, g_ref, lb_ref, an_ref, mall_ref, lvl_ref, og_ref, s_ref, st_scr,
                 *, c, nlev):
    ci = pl.program_id(1)

    @pl.when(ci == 0)
    def _():
        st_scr[...] = jnp.zeros_like(st_scr)

    lvl = lvl_ref[...]
    an = an_ref[...]
    seqs = range(q_ref.shape[0])
    hsl = [slice(h * A_DK, (h + 1) * A_DK) for h in range(A_HEADS)]
    units = [(n, h) for n in seqs for h in range(A_HEADS)]
    qh, kk, e = [], [], []
    for n in seqs:
        q = q_ref[n].astype(F32)
        qh.append(q * _sigmoid(q))
        logf, k_n = _hgrn_gates(f_ref[n], lb_ref[...])
        kk.append(k_n)
        hi = logf.astype(BF16)
        mid = (logf - hi.astype(F32)).astype(BF16)
        d2 = _dot(mall_ref[...], jnp.concatenate([hi, mid], axis=1))
        e.append(jnp.exp(d2[:, :A_K] + d2[:, A_K:]))
    a = {u: jnp.zeros((c, c), F32) for u in units}
    for l in range(nlev + 1):
        for n, h in units:
            if l < nlev:
                el = e[n][l * c:(l + 1) * c, hsl[h]]
                x, y = qh[n][:, hsl[h]] * el, kk[n][:, hsl[h]] * el
            else:
                x, y = qh[n][:, hsl[h]], kk[n][:, hsl[h]]
            p = lax.dot_general(x.astype(BF16), y.astype(BF16), _NT, preferred_element_type=F32)
            a[n, h] = jnp.where(lvl == l, p, a[n, h])
    eb = [e[n][nlev * c:(nlev + 1) * c] for n in seqs]
    ke = [(kk[n] * e[n][(nlev + 1) * c:]).astype(BF16) for n in seqs]
    qe = [(qh[n] * eb[n]).astype(BF16) for n in seqs]
    v = [i_ref[n].astype(BF16) for n in seqs]
    gate = [_sigmoid(g_ref[n].astype(F32)) for n in seqs]
    for n, h in units:
        st = st_scr[n, h]
        o = _dot(a[n, h].astype(BF16), v[n][:, hsl[h]]) + lax.dot_general(
            qe[n][:, hsl[h]], st.astype(BF16), _NT, preferred_element_type=F32)
        st_new = st * eb[n][c - 1:c, hsl[h]] + lax.dot_general(v[n][:, hsl[h]], ke[n][:, hsl[h]], _TN,
                                                                preferred_element_type=F32)
        st_scr[n, h] = st_new
        og_ref[n, :, hsl[h]] = (_rms(o) * an * gate[n][:, hsl[h]]).astype(og_ref.dtype)

        @pl.when(ci == pl.num_programs(1) - 1)
        def _():
            s_ref[n, h] = st_new.T


def _hgrn_prompt(hh, lb, a_norm, batch, seq):
    c = _tile(seq, HGRN_CHUNK)
    mall, lvl, nlev = _hgrn_tables(c)
    nc = seq // c
    grp = HGRN_SEQS if batch % HGRN_SEQS == 0 else 1
    hh3 = hh.reshape(batch, seq, 4 * A_K)
    col = lambda k: pl.BlockSpec((grp, c, A_K), lambda b, i, k=k: (b, i, k))
    const = lambda shape: pl.BlockSpec(shape, lambda b, i: (0,) * len(shape))
    og, state = pl.pallas_call(
        functools.partial(_hgrn_kernel, c=c, nlev=nlev),
        grid=(batch // grp, nc),
        in_specs=[col(0), col(1), col(2), col(3), const((1, A_K)), const((1, A_DV)),
                  const(mall.shape), const(lvl.shape)],
        out_specs=[pl.BlockSpec((grp, c, A_V), lambda b, i: (b, i, 0)),
                   pl.BlockSpec((grp, A_HEADS, A_DK, A_DV), lambda b, i: (b, 0, 0, 0))],
        out_shape=[jax.ShapeDtypeStruct((batch, seq, A_V), BF16),
                   jax.ShapeDtypeStruct((batch, A_HEADS, A_DK, A_DV), F32)],
        scratch_shapes=[pltpu.VMEM((grp, A_HEADS, A_DV, A_DK), F32)],
        compiler_params=_params("parallel", "arbitrary"),
    )(hh3, hh3, hh3, hh3, lb, a_norm, jnp.asarray(mall, BF16), jnp.asarray(lvl))
    return og.reshape(batch * seq, A_V), state


HGRN_STEP_ROWS = 8


def _hgrn_step_kernel(h_ref, s0_ref, lb_ref, an_ref, og_ref, s_ref):
    rows = HGRN_STEP_ROWS
    rid = lax.broadcasted_iota(jnp.int32, (rows, A_DK), 0)
    ones = jnp.ones((rows, A_DV), F32)
    an = an_ref[...]
    hp = lax.Precision.HIGHEST
    for h in range(A_HEADS):
        sl = lambda k: slice(k * A_K + h * A_DK, k * A_K + (h + 1) * A_DK)
        q = h_ref[:, sl(0)]
        qh = q * _sigmoid(q)
        f = h_ref[:, sl(1)]
        lb = lb_ref[:, h * A_DK:(h + 1) * A_DK]
        fg = lb + (1.0 - lb) * _sigmoid(f)
        kk = (1.0 - lb) * _sigmoid(-f)
        v = h_ref[:, sl(2)]
        o = jnp.zeros((rows, A_DV), F32)
        for r in range(rows):
            pick = rid == r
            fcol = lax.dot_general(jnp.where(pick, fg, 0.0), ones, _TN, precision=hp, preferred_element_type=F32)
            upd = lax.dot_general(jnp.where(pick, kk, 0.0), v, _TN, precision=hp, preferred_element_type=F32)
            s_new = fcol * s0_ref[0, r, h] + upd
            s_ref[r, h] = s_new
            o = o + jnp.dot(jnp.where(pick, qh, 0.0), s_new, precision=hp, preferred_element_type=F32)
        og_ref[:, h * A_DV:(h + 1) * A_DV] = (_rms(o) * an * _sigmoid(h_ref[:, sl(3)])).astype(og_ref.dtype)


def _hgrn_step(hh, state, layer, lb, a_norm):
    bs = hh.shape[0]
    rows = HGRN_STEP_ROWS
    return pl.pallas_call(
        _hgrn_step_kernel, grid=(bs // rows,),
        in_specs=[pl.BlockSpec((rows, 4 * A_K), lambda i: (i, 0)),
                  pl.BlockSpec((1, rows, A_HEADS, A_DK, A_DV), lambda i: (layer, i, 0, 0, 0)),
                  pl.BlockSpec((1, A_K), lambda i: (0, 0)), pl.BlockSpec((1, A_DV), lambda i: (0, 0))],
        out_specs=[pl.BlockSpec((rows, A_V), lambda i: (i, 0)),
                   pl.BlockSpec((rows, A_HEADS, A_DK, A_DV), lambda i: (i, 0, 0, 0))],
        out_shape=[jax.ShapeDtypeStruct((bs, A_V), BF16),
                   jax.ShapeDtypeStruct((bs, A_HEADS, A_DK, A_DV), F32)],
        compiler_params=_params("parallel"),
    )(hh, state, lb, a_norm)


POOL_HALO = 16


def _pool_project(m_groups, pw_ref, ps_ref, o_ref):
    for gi, m in enumerate(m_groups):
        sl = slice(gi * POOL_GW, (gi + 1) * POOL_GW)
        o_ref[:, sl] = (_dot(m.astype(BF16), pw_ref[gi]) * ps_ref[:, sl]).astype(o_ref.dtype)


def _pool_kernel(p_ref, pw_ref, ps_ref, o_ref, prev_scr, *, t):
    ti = pl.program_id(1)

    @pl.when(ti == 0)
    def _():
        prev_scr[...] = jnp.zeros_like(prev_scr)

    z = p_ref[...]
    zext = jnp.concatenate([prev_scr[...], z], axis=0)
    prev_scr[...] = z[t - POOL_HALO:, :]
    pos1 = (ti * t + 1 + lax.broadcasted_iota(jnp.int32, (t, 1), 0)).astype(F32)
    s = zext
    groups = []
    for gi, w in enumerate(POOL_WINDOWS):
        s = s[:, (POOL_GW if gi else 0):]
        s = s + pltpu.roll(s, w // 2, 0)
        sl = slice(gi * POOL_GW, (gi + 1) * POOL_GW)
        groups.append(s[POOL_HALO:, :POOL_GW] / jnp.minimum(pos1, float(w)) - z[:, sl])
    _pool_project(groups, pw_ref, ps_ref, o_ref)


def _pool_prompt(p, pool_w, pool_scale, batch, seq):
    t = _tile(seq, 512)
    nt = seq // t
    return pl.pallas_call(
        functools.partial(_pool_kernel, t=t), grid=(batch, nt),
        in_specs=[pl.BlockSpec((t, POOL_W), lambda b, i: (b * nt + i, 0)),
                  pl.BlockSpec(pool_w.shape, lambda b, i: (0, 0, 0)),
                  pl.BlockSpec((1, POOL_W), lambda b, i: (0, 0))],
        out_specs=pl.BlockSpec((t, POOL_W), lambda b, i: (b * nt + i, 0)),
        out_shape=jax.ShapeDtypeStruct((batch * seq, POOL_W), BF16),
        scratch_shapes=[pltpu.VMEM((POOL_HALO, POOL_W), F32)],
        compiler_params=_params("parallel", "arbitrary"),
    )(p, pool_w, pool_scale)


def _pool_step_kernel(p_ref, buf_ref, pw_ref, ps_ref, o_ref, nb_ref, *, past_len):
    z = p_ref[...]
    groups = []
    for gi, w in enumerate(POOL_WINDOWS):
        s = z[:, gi * POOL_GW:(gi + 1) * POOL_GW]
        for r in range(POOL_BUF - (w - 1), POOL_BUF):
            s = s + buf_ref[0, :, r * POOL_W + gi * POOL_GW:r * POOL_W + (gi + 1) * POOL_GW]
        groups.append(s / float(min(past_len + 1, w)) - z[:, gi * POOL_GW:(gi + 1) * POOL_GW])
    _pool_project(groups, pw_ref, ps_ref, o_ref)
    nb_ref[:, :(POOL_BUF - 1) * POOL_W] = buf_ref[0, :, POOL_W:]
    nb_ref[:, (POOL_BUF - 1) * POOL_W:] = z


def _pool_step(p, state_pool, layer, pool_w, pool_scale, past_len):
    bs = p.shape[0]
    flat = POOL_BUF * POOL_W
    pm, nb = pl.pallas_call(
        functools.partial(_pool_step_kernel, past_len=past_len), grid=(1,),
        in_specs=[pl.BlockSpec((bs, POOL_W), lambda i: (0, 0)),
                  pl.BlockSpec((1, bs, flat), lambda i: (layer, 0, 0)),
                  pl.BlockSpec(pool_w.shape, lambda i: (0, 0, 0)),
                  pl.BlockSpec((1, POOL_W), lambda i: (0, 0))],
        out_specs=[pl.BlockSpec((bs, POOL_W), lambda i: (0, 0)),
                   pl.BlockSpec((bs, flat), lambda i: (0, 0))],
        out_shape=[jax.ShapeDtypeStruct((bs, POOL_W), BF16),
                   jax.ShapeDtypeStruct((bs, flat), F32)],
        compiler_params=_params("arbitrary"),
    )(p, state_pool.reshape(state_pool.shape[0], bs, flat), pool_w, pool_scale)
    return pm, nb.reshape(bs, POOL_BUF, POOL_W)


MLA_IN = Q_LORA + KV_LORA + 2 * ROPE
Q_HEAD_COLS = NOPE + 2 * ROPE


def _rope(pair, cs):
    prod = pair * cs
    return prod[:, :ROPE] + prod[:, ROPE:]


def _mla_prep_kernel(h_ref, cs_ref, qn_ref, kn_ref, wuq_ref, wuk_ref, ckv_ref, kr_ref, kcat_ref, qcat_ref):
    cs = cs_ref[...]
    c_kv = _rms(h_ref[:, Q_LORA:Q_LORA + KV_LORA]) * kn_ref[...]
    k_rope = _rope(h_ref[:, Q_LORA + KV_LORA:], cs)
    ckv_ref[...] = c_kv
    kr_ref[...] = k_rope
    kcat_ref[:, :KV_LORA] = c_kv.astype(BF16)
    kcat_ref[:, KV_LORA:] = k_rope.astype(BF16)
    cq = (_rms(h_ref[:, :Q_LORA]) * qn_ref[...]).astype(BF16)
    qf = _dot(cq, wuq_ref[...])
    qscale = ATTN_SCALE * LOG2E
    for h in range(M_HEADS):
        base = h * Q_HEAD_COLS
        q_lat = _dot(qf[:, base:base + NOPE].astype(BF16), wuk_ref[h])
        q_rope = _rope(qf[:, base + NOPE:base + Q_HEAD_COLS], cs)
        qcat_ref[0, h, :, :KV_LORA] = (q_lat * qscale).astype(BF16)
        qcat_ref[0, h, :, KV_LORA:] = (q_rope * qscale).astype(BF16)


def _mla_prep(hm, cs, q_norm, kv_norm, wuq, wuk_t, batch, seq):
    t = _tile(seq, 512)
    nt = seq // t
    n = batch * seq
    row = lambda w: pl.BlockSpec((t, w), lambda b, i: (b * nt + i, 0))
    const = lambda shape: pl.BlockSpec(shape, lambda b, i: (0,) * len(shape))
    return pl.pallas_call(
        _mla_prep_kernel, grid=(batch, nt),
        in_specs=[row(MLA_IN), pl.BlockSpec((t, 2 * ROPE), lambda b, i: (i, 0)),
                  const((1, Q_LORA)), const((1, KV_LORA)), const(wuq.shape), const(wuk_t.shape)],
        out_specs=[row(KV_LORA), row(ROPE), row(QK_CAT),
                   pl.BlockSpec((1, M_HEADS, t, QK_CAT), lambda b, i: (b, 0, i, 0))],
        out_shape=[jax.ShapeDtypeStruct((n, KV_LORA), F32), jax.ShapeDtypeStruct((n, ROPE), F32),
                   jax.ShapeDtypeStruct((n, QK_CAT), BF16),
                   jax.ShapeDtypeStruct((batch, M_HEADS, seq, QK_CAT), BF16)],
        compiler_params=_params("parallel", "parallel"),
    )(hm, cs, q_norm, kv_norm, wuq, wuk_t)


LANES = 128
FLASH_SUB = 256


def _lane_tile(x, n):
    return jnp.concatenate([x] * n, axis=1)


def _flash_kernel(q_ref, k_ref, wuv_ref, o_ref, m_scr, l_scr, acc_scr, *, bq, sub):
    qi = pl.program_id(1)
    m_scr[...] = jnp.full_like(m_scr, -jnp.inf)
    l_scr[...] = jnp.zeros_like(l_scr)
    acc_scr[...] = jnp.zeros_like(acc_scr)

    def step(j, masked):
        kblk = k_ref[0, pl.ds(pl.multiple_of(j * bq, bq), bq), :]
        vblk = kblk[:, :KV_LORA]
        chains = [(h, u) for h in range(M_HEADS) for u in range(bq // sub)]

        def width(u):
            return (u + 1) * sub if masked else bq

        def scores(h, u):
            q = q_ref[0, h, u * sub:(u + 1) * sub, :]
            return lax.dot_general(q, kblk[:width(u)], _NT, preferred_element_type=F32)

        s_next = scores(*chains[0])
        for ci, (h, u) in enumerate(chains):
            s = s_next
            if ci + 1 < len(chains):
                s_next = scores(*chains[ci + 1])
            rows = pl.ds(h * bq + u * sub, sub)
            nk = width(u)
            if masked:
                tq = u * sub + lax.broadcasted_iota(jnp.int32, (sub, nk), 0)
                tk = lax.broadcasted_iota(jnp.int32, (sub, nk), 1)
                s = jnp.where(tk <= tq, s, -jnp.inf)
            m_prev = m_scr[rows, :]
            m_new = jnp.maximum(m_prev, jnp.max(s, axis=1, keepdims=True))
            alpha = jnp.exp2(m_prev - m_new)
            p = jnp.exp2(s - _lane_tile(m_new, nk // LANES))
            l_scr[rows, :] = alpha * l_scr[rows, :] + jnp.sum(p, axis=1, keepdims=True)
            acc_scr[rows, :] = (_lane_tile(alpha, KV_LORA // LANES) * acc_scr[rows, :]
                                + _dot(p.astype(BF16), vblk[:nk]))
            m_scr[rows, :] = m_new

    def body(j, carry):
        step(j, False)
        return carry

    lax.fori_loop(0, qi, body, 0)
    step(qi, True)
    for h in range(M_HEADS):
        rows = pl.ds(h * bq, bq)
        inv_l = _lane_tile(1.0 / l_scr[rows, :], KV_LORA // LANES)
        o_ref[:, h * M_VD:(h + 1) * M_VD] = _dot((acc_scr[rows, :] * inv_l).astype(BF16),
                                                 wuv_ref[h]).astype(o_ref.dtype)


def _flash_prompt(qcat, kcat, wuv, batch, seq):
    bq = _tile(seq, 512)
    nq = seq // bq
    rows = M_HEADS * bq
    return pl.pallas_call(
        functools.partial(_flash_kernel, bq=bq, sub=min(FLASH_SUB, bq)), grid=(batch, nq),
        in_specs=[pl.BlockSpec((1, M_HEADS, bq, QK_CAT), lambda b, i: (b, 0, i, 0)),
                  pl.BlockSpec((1, seq, QK_CAT), lambda b, i: (b, 0, 0)),
                  pl.BlockSpec(wuv.shape, lambda b, i: (0, 0, 0))],
        out_specs=pl.BlockSpec((bq, M_V), lambda b, i: (b * nq + i, 0)),
        out_shape=jax.ShapeDtypeStruct((batch * seq, M_V), BF16),
        scratch_shapes=[pltpu.VMEM((rows, LANES), F32), pltpu.VMEM((rows, LANES), F32),
                        pltpu.VMEM((rows, KV_LORA), F32)],
        compiler_params=_params("parallel", "arbitrary"),
    )(qcat, kcat.reshape(batch, seq, QK_CAT), wuv)


Q_ROWS = 8
PAGES_PER_STEP = 128
DECODE_SUB_KEYS = 1024


def _decode_kernel(pt_ref, q_ref, cnew_ref, rnew_ref, wuv_ref, ckv_hbm, kr_hbm, o_ref,
                   kbuf, rbuf, sem, m_scr, l_scr, acc_scr, *, layer, ppc, page):
    b = pl.program_id(0)
    c = pl.program_id(1)
    nb = pl.num_programs(0)
    nch = pl.num_programs(1)
    step = b * nch + c
    slot = step % 2

    def copies(bb, cc, sl, j):
        pg = pt_ref[bb, cc * ppc + j]
        keys = pl.ds(j * page, page)
        return (pltpu.make_async_copy(ckv_hbm.at[layer, pg], kbuf.at[sl, keys], sem.at[0, sl]),
                pltpu.make_async_copy(kr_hbm.at[layer, pg], rbuf.at[sl, :, keys], sem.at[1, sl]))

    def issue(bb, cc, sl):
        for j in range(ppc):
            for cp in copies(bb, cc, sl, j):
                cp.start()

    @pl.when(step == 0)
    def _():
        issue(b, c, slot)

    @pl.when(step + 1 < nb * nch)
    def _():
        nxt = step + 1
        issue(nxt // nch, nxt % nch, 1 - slot)

    q = q_ref[0]
    q_lat = q[:, :KV_LORA]
    q_rope = q[:, KV_LORA:]

    @pl.when(c == 0)
    def _():
        c_new = cnew_ref[0]
        m_scr[...] = (jnp.sum(q_lat.astype(F32) * c_new, axis=1, keepdims=True)
                      + jnp.sum(q_rope.astype(F32) * rnew_ref[0], axis=1, keepdims=True))
        l_scr[...] = jnp.ones_like(l_scr)
        acc_scr[...] = jnp.broadcast_to(c_new, acc_scr.shape)

    for j in range(ppc):
        for cp in copies(b, c, slot, j):
            cp.wait()

    sub = min(DECODE_SUB_KEYS, ppc * page)
    parts = [(m_scr[...], l_scr[...], acc_scr[...])]
    n_sub = ppc * page // sub
    kcs = [kbuf[slot, pl.ds(u * sub, sub), :].astype(BF16) for u in range(n_sub)]
    ss = [lax.dot_general(q_lat, kcs[u], _NT, preferred_element_type=F32)
          + _dot(q_rope, rbuf[slot, :, pl.ds(u * sub, sub)].astype(BF16)) for u in range(n_sub)]
    ms = [jnp.max(s, axis=1, keepdims=True) for s in ss]
    ps = [jnp.exp2(s - m) for s, m in zip(ss, ms)]
    for u in range(n_sub):
        parts.append((ms[u], jnp.sum(ps[u], axis=1, keepdims=True), _dot(ps[u].astype(BF16), kcs[u])))
    m_new = functools.reduce(jnp.maximum, [m for m, _, _ in parts])
    weights = [jnp.exp2(m - m_new) for m, _, _ in parts]
    m_scr[...] = m_new
    l_scr[...] = sum(w * l for w, (_, l, _) in zip(weights, parts))
    acc_scr[...] = sum(w * a for w, (_, _, a) in zip(weights, parts))

    @pl.when(c == nch - 1)
    def _():
        o_lat = (acc_scr[...] / l_scr[...]).astype(BF16)
        for h in range(M_HEADS):
            o_ref[0, :, h * M_VD:(h + 1) * M_VD] = _dot(o_lat, wuv_ref[h])[h:h + 1].astype(o_ref.dtype)


def _mla_decode(page_table, q8, c_new, r_new, wuv, cache_ckv, cache_krope_t, layer):
    bs, n_pages = page_table.shape
    page = cache_ckv.shape[2]
    ppc = PAGES_PER_STEP if n_pages % PAGES_PER_STEP == 0 else n_pages
    nch = n_pages // ppc
    grid_spec = pltpu.PrefetchScalarGridSpec(
        num_scalar_prefetch=1, grid=(bs, nch),
        in_specs=[pl.BlockSpec((1, Q_ROWS, QK_CAT), lambda b, c, pt: (b, 0, 0)),
                  pl.BlockSpec((1, 1, KV_LORA), lambda b, c, pt: (b, 0, 0)),
                  pl.BlockSpec((1, 1, ROPE), lambda b, c, pt: (b, 0, 0)),
                  pl.BlockSpec(wuv.shape, lambda b, c, pt: (0, 0, 0)),
                  pl.BlockSpec(memory_space=pl.ANY), pl.BlockSpec(memory_space=pl.ANY)],
        out_specs=pl.BlockSpec((1, 1, M_V), lambda b, c, pt: (b, 0, 0)),
        scratch_shapes=[pltpu.VMEM((2, ppc * page, KV_LORA), F32), pltpu.VMEM((2, ROPE, ppc * page), F32),
                        pltpu.SemaphoreType.DMA((2, 2)),
                        pltpu.VMEM((Q_ROWS, 1), F32), pltpu.VMEM((Q_ROWS, 1), F32),
                        pltpu.VMEM((Q_ROWS, KV_LORA), F32)])
    return pl.pallas_call(
        functools.partial(_decode_kernel, layer=layer, ppc=ppc, page=page),
        grid_spec=grid_spec, out_shape=jax.ShapeDtypeStruct((bs, 1, M_V), BF16),
        compiler_params=_params("arbitrary", "arbitrary"),
    )(page_table, q8, c_new, r_new, wuv, cache_ckv, cache_krope_t)


def _merge_kernel(og_ref, pm_ref, ov_ref, gt_ref, x_ref, wa_ref, wp_ref, wm_ref, wo_ref, g_ref, b_ref,
                  o_ref, ob_ref, *, alpha):
    d = D_MODEL
    merged = (gt_ref[:, :d].astype(F32) * _dot(og_ref[...], wa_ref[...])
              + gt_ref[:, d:2 * d].astype(F32) * _dot(pm_ref[...], wp_ref[...])
              + gt_ref[:, 2 * d:].astype(F32) * _dot(ov_ref[...], wm_ref[...]))
    y = alpha * x_ref[...] + _dot(merged.astype(BF16), wo_ref[...])
    out = _layernorm(y, g_ref[...], b_ref[...])
    o_ref[...] = out
    ob_ref[...] = out.astype(BF16)


def _merge(og, pm, ov, gates, x, wa, wp, wm, wo, g, b, alpha):
    n = x.shape[0]
    t = _tile(n, 512)
    row = lambda w: pl.BlockSpec((t, w), lambda i: (i, 0))
    const = lambda a: pl.BlockSpec(a.shape, lambda i: (0,) * a.ndim)
    return pl.pallas_call(
        functools.partial(_merge_kernel, alpha=alpha), grid=(n // t,),
        in_specs=[row(A_V), row(POOL_W), row(M_V), row(N_BRANCH * D_MODEL), row(D_MODEL),
                  const(wa), const(wp), const(wm), const(wo), const(g), const(b)],
        out_specs=[row(D_MODEL), row(D_MODEL)],
        out_shape=[jax.ShapeDtypeStruct((n, D_MODEL), F32), jax.ShapeDtypeStruct((n, D_MODEL), BF16)],
        compiler_params=_params("parallel"),
    )(og, pm, ov, gates, x, wa, wp, wm, wo, g, b)


FF_BLOCK = D_FF // 2


def _swiglu_block(xb, w1, w3, w2):
    a = _dot(xb, w1)
    return _dot((a * _sigmoid(a) * _dot(xb, w3)).astype(BF16), w2)


def _ffn_kernel(xb_ref, x_ref, w1_ref, w3_ref, w2_ref, g_ref, b_ref, o_ref, ob_ref, acc_scr, *, alpha):
    j = pl.program_id(1)

    @pl.when(j == 0)
    def _():
        acc_scr[...] = jnp.zeros_like(acc_scr)

    acc_scr[...] += _swiglu_block(xb_ref[...], w1_ref[...], w3_ref[...], w2_ref[...])

    @pl.when(j == pl.num_programs(1) - 1)
    def _():
        out = _layernorm(alpha * x_ref[...] + acc_scr[...], g_ref[...], b_ref[...])
        o_ref[...] = out
        ob_ref[...] = out.astype(BF16)


def _ffn(xb, x, w1, w3, w2, g, b, alpha):
    n = x.shape[0]
    t = _tile(n, 512)
    nf = D_FF // FF_BLOCK
    row = lambda: pl.BlockSpec((t, D_MODEL), lambda i, j: (i, 0))
    vec = lambda: pl.BlockSpec((1, D_MODEL), lambda i, j: (0, 0))
    return pl.pallas_call(
        functools.partial(_ffn_kernel, alpha=alpha), grid=(n // t, nf),
        in_specs=[row(), row(), pl.BlockSpec((D_MODEL, FF_BLOCK), lambda i, j: (0, j)),
                  pl.BlockSpec((D_MODEL, FF_BLOCK), lambda i, j: (0, j)),
                  pl.BlockSpec((FF_BLOCK, D_MODEL), lambda i, j: (j, 0)), vec(), vec()],
        out_specs=[row(), row()],
        out_shape=[jax.ShapeDtypeStruct((n, D_MODEL), F32), jax.ShapeDtypeStruct((n, D_MODEL), BF16)],
        scratch_shapes=[pltpu.VMEM((t, D_MODEL), F32)],
        compiler_params=_params("parallel", "arbitrary"),
    )(xb, x, w1, w3, w2, g, b)


TOP_K = 2


def _router_kernel(x_ref, rt_ref, idx_ref, w_ref):
    logits = lax.dot_general(rt_ref[...], x_ref[...], _NT, precision=lax.Precision.HIGHEST,
                             preferred_element_type=F32)
    eid = lax.broadcasted_iota(jnp.int32, logits.shape, 0)
    m1 = jnp.max(logits, axis=0, keepdims=True)
    i1 = jnp.min(jnp.where(logits == m1, eid, N_EXPERTS), axis=0, keepdims=True)
    rest = jnp.where(eid == i1, -jnp.inf, logits)
    m2 = jnp.max(rest, axis=0, keepdims=True)
    i2 = jnp.min(jnp.where(rest == m2, eid, N_EXPERTS), axis=0, keepdims=True)
    e2 = jnp.exp(m2 - m1)
    idx_ref[...] = jnp.concatenate([i1, i2], axis=0)
    w_ref[...] = jnp.concatenate([1.0 / (1.0 + e2), e2 / (1.0 + e2)], axis=0)


def _router(x, router_t):
    n = x.shape[0]
    t = _tile(n, 1024)
    return pl.pallas_call(
        _router_kernel, grid=(n // t,),
        in_specs=[pl.BlockSpec((t, D_MODEL), lambda i: (i, 0)),
                  pl.BlockSpec((N_EXPERTS, D_MODEL), lambda i: (0, 0))],
        out_specs=[pl.BlockSpec((TOP_K, t), lambda i: (0, i)), pl.BlockSpec((TOP_K, t), lambda i: (0, i))],
        out_shape=[jax.ShapeDtypeStruct((TOP_K, n), jnp.int32), jax.ShapeDtypeStruct((TOP_K, n), F32)],
        compiler_params=_params("parallel"),
    )(x, router_t)


MOE_ROWS = 256
COMBINE_ROWS = 256


def _row_copy(src_hbm, row, dst_buf, slot, i, sem):
    return pltpu.make_async_copy(src_hbm.at[pl.ds(row, 1)], dst_buf.at[slot, pl.ds(i, 1)], sem.at[slot])


def _gather_rows(step, n_steps, cur_ref, nxt_ref, src_hbm, buf, sem, n_rows):
    slot = step % 2

    @pl.when(step == 0)
    def _():
        for i in range(n_rows):
            _row_copy(src_hbm, cur_ref[0, 0, i], buf, 0, i, sem).start()

    for i in range(n_rows):
        _row_copy(src_hbm, nxt_ref[0, 0, i], buf, 1 - slot, i, sem).start()
    for i in range(n_rows):
        _row_copy(src_hbm, 0, buf, slot, i, sem).wait()

    @pl.when(step == n_steps - 1)
    def _():
        for i in range(n_rows):
            _row_copy(src_hbm, 0, buf, 1 - slot, i, sem).wait()

    return slot


def _experts_kernel(be_ref, cur_ref, nxt_ref, x_hbm, w1_ref, w3_ref, w2_ref, y_ref, xbuf, sem, *, rb):
    del be_ref
    slot = _gather_rows(pl.program_id(0), pl.num_programs(0), cur_ref, nxt_ref, x_hbm, xbuf, sem, rb)
    y_ref[...] = _swiglu_block(xbuf[slot].astype(BF16), w1_ref[0], w3_ref[0], w2_ref[0])


def _combine_kernel(cur_ref, nxt_ref, x_ref, w_ref, y_hbm, g_ref, b_ref, o_ref, ob_ref, ybuf, sem, *, tt, alpha):
    slot = _gather_rows(pl.program_id(0), pl.num_programs(0), cur_ref, nxt_ref, y_hbm, ybuf, sem, TOP_K * tt)
    w = w_ref[...]
    ffn = w[:, 0:1] * ybuf[slot, :tt, :] + w[:, 1:2] * ybuf[slot, tt:, :]
    out = _layernorm(alpha * x_ref[...] + ffn, g_ref[...], b_ref[...])
    o_ref[...] = out
    ob_ref[...] = out.astype(BF16)


def _moe_plan(idx_t, rb):
    n = idx_t.shape[1]
    n_assign = TOP_K * n
    p_total = n_assign + N_EXPERTS * rb
    ex = idx_t.reshape(n_assign)
    tok = jnp.tile(jnp.arange(n, dtype=jnp.int32), TOP_K)
    order = jnp.argsort(ex, stable=True)
    ex_sorted = ex[order]
    counts = jnp.sum((ex[None, :] == jnp.arange(N_EXPERTS, dtype=jnp.int32)[:, None]).astype(jnp.int32), axis=1)
    padded = ((counts + rb - 1) // rb) * rb
    group_end = jnp.cumsum(padded)
    group_start = group_end - padded
    first = jnp.cumsum(counts) - counts
    dest = group_start[ex_sorted] + jnp.arange(n_assign, dtype=jnp.int32) - first[ex_sorted]
    position = dest[jnp.argsort(order)]
    block_start = jnp.arange(p_total // rb, dtype=jnp.int32) * rb
    block_expert = jnp.minimum(jnp.searchsorted(group_end, block_start, side='right'),
                               N_EXPERTS - 1).astype(jnp.int32)
    row_expert = jnp.repeat(block_expert, rb)
    rank = jnp.arange(p_total, dtype=jnp.int32) - group_start[row_expert]
    source = jnp.clip(first[row_expert] + rank, 0, n_assign - 1)
    row_token = jnp.where((rank >= 0) & (rank < counts[row_expert]), tok[order][source], 0)
    return row_token.astype(jnp.int32), block_expert, position.reshape(TOP_K, n)


def _moe(x, idx_t, w_t, w1, w3, w2, g, b, alpha):
    n = x.shape[0]
    rb = min(MOE_ROWS, n)
    row_token, block_expert, position = _moe_plan(idx_t, rb)
    nb = row_token.shape[0] // rb
    rows3 = row_token.reshape(nb, 1, rb)
    smem = lambda width, nxt, last: pl.BlockSpec(
        (1, 1, width), lambda i, *_: (jnp.minimum(i + nxt, last), 0, 0), memory_space=pltpu.SMEM)
    weight = lambda shape: pl.BlockSpec((1,) + shape, lambda i, be: (be[i], 0, 0))
    y = pl.pallas_call(
        functools.partial(_experts_kernel, rb=rb),
        grid_spec=pltpu.PrefetchScalarGridSpec(
            num_scalar_prefetch=1, grid=(nb,),
            in_specs=[smem(rb, 0, nb - 1), smem(rb, 1, nb - 1), pl.BlockSpec(memory_space=pl.ANY),
                      weight((D_MODEL, D_FF)), weight((D_MODEL, D_FF)), weight((D_FF, D_MODEL))],
            out_specs=pl.BlockSpec((rb, D_MODEL), lambda i, be: (i, 0)),
            scratch_shapes=[pltpu.VMEM((2, rb, D_MODEL), F32), pltpu.SemaphoreType.DMA((2,))]),
        out_shape=jax.ShapeDtypeStruct((nb * rb, D_MODEL), F32),
        compiler_params=_params("arbitrary"),
    )(block_expert, rows3, rows3, x, w1, w3, w2)

    tt = min(COMBINE_ROWS, n)
    nt = n // tt
    pos3 = position.reshape(TOP_K, nt, tt).transpose(1, 0, 2).reshape(nt, 1, TOP_K * tt)
    row = lambda width: pl.BlockSpec((tt, width), lambda i: (i, 0))
    vec = lambda: pl.BlockSpec((1, D_MODEL), lambda i: (0, 0))
    return pl.pallas_call(
        functools.partial(_combine_kernel, tt=tt, alpha=alpha), grid=(nt,),
        in_specs=[smem(TOP_K * tt, 0, nt - 1), smem(TOP_K * tt, 1, nt - 1), row(D_MODEL), row(TOP_K),
                  pl.BlockSpec(memory_space=pl.ANY), vec(), vec()],
        out_specs=[row(D_MODEL), row(D_MODEL)],
        out_shape=[jax.ShapeDtypeStruct((n, D_MODEL), F32), jax.ShapeDtypeStruct((n, D_MODEL), BF16)],
        scratch_shapes=[pltpu.VMEM((2, TOP_K * tt, D_MODEL), F32), pltpu.SemaphoreType.DMA((2,))],
        compiler_params=_params("arbitrary"),
    )(pos3, pos3, x, w_t.T, y, g, b)


def _swap_halves_cols(w):
    half = w.shape[-1] // 2
    return jnp.concatenate([w[..., half:], w[..., :half]], axis=-1)


def _rope_table(pos):
    half = ROPE // 2
    inv = ROPE_BASE ** (-jnp.arange(half, dtype=F32) / half)
    ang = pos.astype(F32)[:, None] * inv[None, :]
    c, s = jnp.cos(ang), jnp.sin(ang)
    return jnp.concatenate([c, c, -s, s], axis=-1)


def _layer_weights(l, w_in, b_gate, lb, hgrn_norm, w_hgrn_out, pool_w, pool_scale, w_pool_out,
                   mla_q_norm, w_uq, mla_kv_norm, w_uk, w_uv, w_mla_out, w_o):
    offs = np.cumsum((0, A_K, A_K, A_V, A_V, POOL_W, Q_LORA, KV_LORA, ROPE))
    wi = w_in[l]
    w_kr = wi[:, offs[7]:offs[8]]
    uq = w_uq[l].reshape(Q_LORA, M_HEADS, NOPE + ROPE)
    uq = jnp.concatenate([uq, _swap_halves_cols(uq[..., NOPE:])], axis=-1)
    return dict(
        w_hgrn=wi[:, :offs[4]].astype(BF16),
        w_pool=wi[:, offs[4]:offs[5]].astype(BF16),
        w_mla=jnp.concatenate([wi[:, offs[5]:offs[8]], _swap_halves_cols(w_kr)], axis=1).astype(BF16),
        w_gate=wi[:, offs[8]:].astype(BF16),
        b_gate=b_gate[l].reshape(1, N_BRANCH * D_MODEL),
        lb=lb[l].reshape(1, A_K),
        a_norm=hgrn_norm[l].reshape(1, A_DV),
        w_a_out=w_hgrn_out[l].astype(BF16),
        pool_w=pool_w[l].astype(BF16),
        pool_scale=pool_scale[l].reshape(1, POOL_W),
        w_p_out=w_pool_out[l].astype(BF16),
        q_norm=mla_q_norm[l].reshape(1, Q_LORA),
        kv_norm=mla_kv_norm[l].reshape(1, KV_LORA),
        w_uq=uq.reshape(Q_LORA, M_HEADS * Q_HEAD_COLS).astype(BF16),
        w_uk_t=jnp.transpose(w_uk[l], (1, 2, 0)).astype(BF16),
        w_uv=jnp.transpose(w_uv[l], (1, 0, 2)).astype(BF16),
        w_m_out=w_mla_out[l].astype(BF16),
        w_o=w_o[l].astype(BF16),
    )


def _in_projections(xb, lw):
    hh = _matmul(xb, lw['w_hgrn'], F32, 1024)
    hp = _matmul(xb, lw['w_pool'], F32, POOL_W)
    hm = _matmul(xb, lw['w_mla'], F32, MLA_IN)
    gates = _matmul(xb, lw['w_gate'], BF16, 1024, bias=lw['b_gate'])
    return hh, hp, hm, gates


def kernel(x_prompt, x_sample, cache_ckv, cache_krope, state_hgrn, state_pool, page_table, w_in, b_gate, hgrn_lb_logits, hgrn_norm, w_hgrn_out, pool_w, pool_scale, w_pool_out, mla_q_norm, w_uq, mla_kv_norm, w_uk, w_uv, w_mla_out, w_o, ln1_g, ln1_b, ln2_g, ln2_b, ffn_w1, ffn_w3, ffn_w2, moe_router, moe_w1, moe_w3, moe_w2):
    bp, lp, _ = x_prompt.shape
    bs, ls, _ = x_sample.shape
    assert ls == 1, "the decode path handles one new token per sequence"
    depth = w_in.shape[0]
    past_len = page_table.shape[1] * cache_ckv.shape[2]
    alpha = (2 * depth) ** 0.25
    npr = bp * lp

    lb_cum = jnp.cumsum(jax.nn.softmax(hgrn_lb_logits.astype(F32), axis=0), axis=0)
    lb = lb_cum - lb_cum[0:1]
    cs_p = _rope_table(jnp.arange(lp, dtype=jnp.int32))
    cs_s = _rope_table(jnp.full((bs,), past_len, jnp.int32))
    vec = lambda a: a.reshape(1, D_MODEL)
    cache_krope_t = jnp.swapaxes(cache_krope, 2, 3)

    xp = x_prompt.reshape(npr, D_MODEL)
    xs = x_sample.reshape(bs, D_MODEL)
    xpb, xsb = xp.astype(BF16), xs.astype(BF16)
    outs = {k: [] for k in ('ckv_p', 'kr_p', 'ckv_s', 'kr_s', 'sh_p', 'sh_s', 'pb_p', 'pb_s')}
    for l in range(depth):
        lw = _layer_weights(l, w_in, b_gate, lb, hgrn_norm, w_hgrn_out, pool_w, pool_scale, w_pool_out,
                            mla_q_norm, w_uq, mla_kv_norm, w_uk, w_uv, w_mla_out, w_o)
        hh, hp, hm, gates = _in_projections(xpb, lw)
        og, s_p = _hgrn_prompt(hh, lw['lb'], lw['a_norm'], bp, lp)
        pm = _pool_prompt(hp, lw['pool_w'], lw['pool_scale'], bp, lp)
        ckv, kr, kcat, qcat = _mla_prep(hm, cs_p, lw['q_norm'], lw['kv_norm'], lw['w_uq'], lw['w_uk_t'], bp, lp)
        ov = _flash_prompt(qcat, kcat, lw['w_uv'], bp, lp)
        xp1, xp1b = _merge(og, pm, ov, gates, xp, lw['w_a_out'], lw['w_p_out'], lw['w_m_out'], lw['w_o'],
                           vec(ln1_g[l]), vec(ln1_b[l]), alpha)
        outs['ckv_p'].append(ckv.reshape(bp, lp, KV_LORA))
        outs['kr_p'].append(kr.reshape(bp, lp, ROPE))
        outs['sh_p'].append(s_p)
        outs['pb_p'].append(hp.reshape(bp, lp, POOL_W)[:, lp - POOL_BUF:])
        hh, hp, hm, gates = _in_projections(xsb, lw)
        og, s_s = _hgrn_step(hh, state_hgrn, l, lw['lb'], lw['a_norm'])
        pm, pb_s = _pool_step(hp, state_pool, l, lw['pool_w'], lw['pool_scale'], past_len)
        ckv, kr, _, qcat = _mla_prep(hm, cs_s, lw['q_norm'], lw['kv_norm'], lw['w_uq'], lw['w_uk_t'], 1, bs)
        q8 = jnp.pad(jnp.transpose(qcat[0], (1, 0, 2)), ((0, 0), (0, Q_ROWS - M_HEADS), (0, 0)))
        ckv = ckv.reshape(bs, 1, KV_LORA)
        kr = kr.reshape(bs, 1, ROPE)
        ov = _mla_decode(page_table, q8, ckv, kr, lw['w_uv'], cache_ckv, cache_krope_t, l).reshape(bs, M_V)
        xs1, xs1b = _merge(og, pm, ov, gates, xs, lw['w_a_out'], lw['w_p_out'], lw['w_m_out'], lw['w_o'],
                           vec(ln1_g[l]), vec(ln1_b[l]), alpha)
        outs['ckv_s'].append(ckv)
        outs['kr_s'].append(kr)
        outs['sh_s'].append(s_s)
        outs['pb_s'].append(pb_s)
        j = l // 2
        if l % 2 == 0:
            w1, w3, w2 = ffn_w1[j].astype(BF16), ffn_w3[j].astype(BF16), ffn_w2[j].astype(BF16)
            xp, xpb = _ffn(xp1b, xp1, w1, w3, w2, vec(ln2_g[l]), vec(ln2_b[l]), alpha)
            xs, xsb = _ffn(xs1b, xs1, w1, w3, w2, vec(ln2_g[l]), vec(ln2_b[l]), alpha)
        else:
            w1, w3, w2 = moe_w1[j].astype(BF16), moe_w3[j].astype(BF16), moe_w2[j].astype(BF16)
            router_t = moe_router[j].T
            xp, xpb = _moe(xp1, *_router(xp1, router_t), w1, w3, w2, vec(ln2_g[l]), vec(ln2_b[l]), alpha)
            xs, xsb = _moe(xs1, *_router(xs1, router_t), w1, w3, w2, vec(ln2_g[l]), vec(ln2_b[l]), alpha)

    st = lambda k: jnp.stack(outs[k])
    return (xp.reshape(bp, lp, D_MODEL), xs.reshape(bs, ls, D_MODEL), st('ckv_p'), st('kr_p'), st('ckv_s'),
            st('kr_s'), st('sh_p'), st('sh_s'), st('pb_p'), st('pb_s'))
```
